```python
import jax, jax.numpy as jnp
from jax import lax
import numpy as np

D_MODEL = 1024
BATCH = 4
SEQ = 8192
DEPTH = 1
DEC_BATCH = 128
DEC_SEQ = 8
PAST_LEN = 8192
PAGE_SIZE = 128

N_META = 16
CHUNK = 128
QBLK = 128
PAD_FRONT = CHUNK - N_META
HD_A = 64
H_A = D_MODEL // 128
H_B = D_MODEL // 256
DK_B = 64
DV_B = 128
MIX_A = H_A * HD_A
MIX_B = H_B * DV_B
MIX = MIX_A + MIX_B
D_FF = 4 * D_MODEL
SPLIT_SIZES = (MIX_A, MIX_A, MIX_A, H_A, H_B * DK_B, H_B * DK_B, MIX_B, MIX_B)
D_IN = 3 * MIX_A + H_A + 2 * H_B * DK_B + 2 * MIX_B
ROPE_BASE = 10000.0
EPS = 1e-6
NEG = -1e30

kernel_name = "fox_retention_parallel_hybrid_step"

f32 = jnp.float32


def rmsnorm(x, g):
    xf = x.astype(f32)
    y = xf * lax.rsqrt(jnp.mean(xf * xf, axis=-1, keepdims=True) + EPS)
    return (y * g.astype(f32)).astype(x.dtype)


def head_layernorm(o):
    mu = jnp.mean(o, axis=-1, keepdims=True)
    var = jnp.mean(jnp.square(o - mu), axis=-1, keepdims=True)
    return (o - mu) * lax.rsqrt(var + EPS)


def rotary(x, pos):
    half = x.shape[-1] // 2
    inv = ROPE_BASE ** (-jnp.arange(half, dtype=f32) / half)
    ang = pos[:, None] * inv[None, :]
    cos = jnp.cos(ang)[None, :, None, :]
    sin = jnp.sin(ang)[None, :, None, :]
    x1, x2 = x[..., :half], x[..., half:]
    return jnp.concatenate([x1 * cos - x2 * sin, x1 * sin + x2 * cos], axis=-1)


def retention_log_gamma():
    return jnp.log1p(-jnp.exp2(-5.0 - jnp.arange(H_B, dtype=f32)))


def project(h, w_in, b_f, pos):
    B, L = h.shape[:2]
    z = h @ w_in
    offs = np.cumsum(SPLIT_SIZES)[:-1].tolist()
    qa, ka, va, fa, qr, kr, vr, gr = jnp.split(z, offs, axis=-1)
    qa = qa.reshape(B, L, H_A, HD_A)
    ka = ka.reshape(B, L, H_A, HD_A)
    va = va.reshape(B, L, H_A, HD_A)
    logf = jax.nn.log_sigmoid(fa.astype(f32) + b_f.astype(f32))
    qr = rotary(qr.reshape(B, L, H_B, DK_B).astype(f32), pos)
    kr = rotary(kr.reshape(B, L, H_B, DK_B).astype(f32), pos) * (DK_B ** -0.5)
    vr = vr.reshape(B, L, H_B, DV_B).astype(f32)
    return qa, ka, va, logf, qr, kr, vr, gr


def retention_chunk(S, q, k, v, lg):
    C = q.shape[1]
    n = jnp.arange(C, dtype=f32)
    rel = n[:, None] - n[None, :]
    dec = jnp.where(rel >= 0, jnp.exp(jnp.maximum(rel, 0.0)[None] * lg[:, None, None]), 0.0)
    q_dec = jnp.exp((n + 1.0)[:, None] * lg[None, :])
    k_dec = jnp.exp((C - 1.0 - n)[:, None] * lg[None, :])
    inner = jnp.einsum('bihd,bjhd->bhij', q, k) * dec[None]
    out = (jnp.einsum('bhij,bjhe->bihe', inner, v)
           + jnp.einsum('bihd,bhde->bihe', q, S) * q_dec[None, :, :, None])
    S_new = (jnp.exp(C * lg)[None, :, None, None] * S
             + jnp.einsum('bjhd,bjhe->bhde', k * k_dec[None, :, :, None], v))
    return S_new, out


def fox_prompt(q, k, v, logf, valid):
    B, L, H, D = q.shape
    nb = L // QBLK
    scale = D ** -0.5
    C = jnp.cumsum(logf, axis=1).transpose(0, 2, 1)
    kidx = jnp.arange(L)
    qb = q.reshape(B, nb, QBLK, H, D).swapaxes(0, 1)
    Cb = C.reshape(B, H, nb, QBLK).transpose(2, 0, 1, 3)
    starts = jnp.arange(nb) * QBLK

    def block(args):
        qi, ci, st = args
        s = (jnp.einsum('bqhd,bkhd->bhqk', qi, k).astype(f32) * scale
             + ci[..., :, None] - C[:, :, None, :])
        qidx = st + jnp.arange(QBLK)
        mask = (kidx[None, :] <= qidx[:, None]) & valid[None, :]
        p = jax.nn.softmax(jnp.where(mask, s, NEG), axis=-1)
        return jnp.einsum('bhqk,bkhd->bqhd', p.astype(v.dtype), v)

    o = lax.map(block, (qb, Cb, starts))
    return o.swapaxes(0, 1).reshape(B, L, H, D)


def fox_sample(q, k, v, logf, kp, vp, lfp):
    T, D = q.shape[1], q.shape[-1]
    P = kp.shape[1]
    scale = D ** -0.5
    Cp = jnp.cumsum(lfp.astype(f32), axis=1)
    Cn = Cp[:, -1:, :] + jnp.cumsum(logf, axis=1)
    Cp = Cp.transpose(0, 2, 1)
    Cn = Cn.transpose(0, 2, 1)
    s_past = (jnp.einsum('bqhd,bkhd->bhqk', q, kp).astype(f32) * scale
              + Cn[..., :, None] - Cp[:, :, None, :])
    s_new = (jnp.einsum('bqhd,bkhd->bhqk', q, k).astype(f32) * scale
             + Cn[..., :, None] - Cn[:, :, None, :])
    causal = jnp.tril(jnp.ones((T, T), dtype=bool))
    s_new = jnp.where(causal[None, None], s_new, NEG)
    p = jax.nn.softmax(jnp.concatenate([s_past, s_new], axis=-1), axis=-1)
    return (jnp.einsum('bhqk,bkhd->bqhd', p[..., :P].astype(vp.dtype), vp)
            + jnp.einsum('bhqk,bkhd->bqhd', p[..., P:].astype(v.dtype), v))


def merge_and_mlp(x, o_a, o_b, gate, w_out, g_post_mix, g_pre_mlp, w_up, w_down, g_post_mlp):
    B, L = x.shape[:2]
    o_b = head_layernorm(o_b).reshape(B, L, MIX_B) * jax.nn.silu(gate.astype(f32))
    mixed = jnp.concatenate([o_a.reshape(B, L, MIX_A).astype(x.dtype), o_b.astype(x.dtype)], axis=-1) @ w_out
    x = x + rmsnorm(mixed, g_post_mix)
    u = jnp.square(jax.nn.relu(rmsnorm(x, g_pre_mlp) @ w_up))
    return x + rmsnorm(u @ w_down, g_post_mlp)


def prompt_layer(x, pos, valid, w_in, b_f, w_out, g_pre_mix, g_post_mix, g_pre_mlp, w_up, w_down, g_post_mlp):
    B, L = x.shape[:2]
    h = rmsnorm(x, g_pre_mix)
    qa, ka, va, logf, qr, kr, vr, gr = project(h, w_in, b_f, pos)
    logf = jnp.where(valid[None, :, None], logf, 0.0)
    kr = jnp.where(valid[None, :, None, None], kr, 0.0)
    o_a = fox_prompt(qa, ka, va, logf, valid)
    nc = L // CHUNK
    chunks = lambda a: a.reshape(B, nc, CHUNK, *a.shape[2:]).swapaxes(0, 1)
    lg = retention_log_gamma()
    S0 = jnp.zeros((B, H_B, DK_B, DV_B), f32)
    S, o_b = lax.scan(lambda S, c: retention_chunk(S, c[0], c[1], c[2], lg), S0,
                      (chunks(qr), chunks(kr), chunks(vr)))
    o_b = o_b.swapaxes(0, 1).reshape(B, L, H_B, DV_B)
    x = merge_and_mlp(x, o_a, o_b, gr, w_out, g_post_mix, g_pre_mlp, w_up, w_down, g_post_mlp)
    return x, ka[:, PAD_FRONT:], va[:, PAD_FRONT:], logf[:, PAD_FRONT:], S


def sample_layer(x, pos, ck, cv, clf, st, page_table, w_in, b_f, w_out, g_pre_mix, g_post_mix,
                 g_pre_mlp, w_up, w_down, g_post_mlp):
    DB = x.shape[0]
    h = rmsnorm(x, g_pre_mix)
    qa, ka, va, logf, qr, kr, vr, gr = project(h, w_in, b_f, pos)
    kp = ck[page_table].reshape(DB, -1, H_A, HD_A)
    vp = cv[page_table].reshape(DB, -1, H_A, HD_A)
    lfp = clf[page_table].reshape(DB, -1, H_A)
    o_a = fox_sample(qa, ka, va, logf, kp, vp, lfp)
    S, o_b = retention_chunk(st.astype(f32), qr, kr, vr, retention_log_gamma())
    x = merge_and_mlp(x, o_a, o_b, gr, w_out, g_post_mix, g_pre_mlp, w_up, w_down, g_post_mlp)
    return x, ka, va, logf, S


def setup_inputs(seed: int = 0) -> dict:
    key = jax.random.key(seed)
    ks = jax.random.split(key, 20)
    n_pages = PAST_LEN // PAGE_SIZE
    used = DEC_BATCH * n_pages
    n_pool = used + max(1, used // 4)
    nrm = jax.random.normal
    x_prompt = nrm(ks[0], (BATCH, SEQ, D_MODEL), f32)
    x_sample = nrm(ks[1], (DEC_BATCH, DEC_SEQ, D_MODEL), f32)
    cache_k = nrm(ks[2], (DEPTH, n_pool, PAGE_SIZE, H_A, HD_A), f32)
    cache_v = nrm(ks[3], (DEPTH, n_pool, PAGE_SIZE, H_A, HD_A), f32)
    cache_logf = jax.nn.log_sigmoid(3.0 + nrm(ks[4], (DEPTH, n_pool, PAGE_SIZE, H_A), f32))
    state_ret = 0.3 * nrm(ks[5], (DEPTH, DEC_BATCH, H_B, DK_B, DV_B), f32)
    page_table = jax.random.permutation(ks[6], n_pool)[:used].reshape(DEC_BATCH, n_pages).astype(jnp.int32)
    meta_tokens = nrm(ks[7], (N_META, D_MODEL), f32)
    gain = lambda k: 1.0 + 0.05 * nrm(k, (DEPTH, D_MODEL), f32)
    g_pre_mix = gain(ks[8])
    w_in = nrm(ks[9], (DEPTH, D_MODEL, D_IN), f32) * D_MODEL ** -0.5
    b_f = 3.0 + 0.5 * nrm(ks[10], (DEPTH, H_A), f32)
    w_out = nrm(ks[11], (DEPTH, MIX, D_MODEL), f32) * MIX ** -0.5
    g_post_mix = gain(ks[12])
    g_pre_mlp = gain(ks[13])
    w_up = nrm(ks[14], (DEPTH, D_MODEL, D_FF), f32) * D_MODEL ** -0.5
    w_down = nrm(ks[15], (DEPTH, D_FF, D_MODEL), f32) * D_FF ** -0.5
    g_post_mlp = gain(ks[16])
    return {"x_prompt": x_prompt, "x_sample": x_sample, "cache_k": cache_k, "cache_v": cache_v,
            "cache_logf": cache_logf, "state_ret": state_ret, "page_table": page_table,
            "meta_tokens": meta_tokens, "g_pre_mix": g_pre_mix, "w_in": w_in, "b_f": b_f,
            "w_out": w_out, "g_post_mix": g_post_mix, "g_pre_mlp": g_pre_mlp, "w_up": w_up,
            "w_down": w_down, "g_post_mlp": g_post_mlp}


def reference(x_prompt, x_sample, cache_k, cache_v, cache_logf, state_ret, page_table, meta_tokens,
              g_pre_mix, w_in, b_f, w_out, g_post_mix, g_pre_mlp, w_up, w_down, g_post_mlp):
    B = x_prompt.shape[0]
    xp = jnp.concatenate([jnp.zeros((B, PAD_FRONT, D_MODEL), x_prompt.dtype),
                          jnp.broadcast_to(meta_tokens.astype(x_prompt.dtype)[None], (B, N_META, D_MODEL)),
                          x_prompt], axis=1)
    L = xp.shape[1]
    pos_p = jnp.arange(L, dtype=f32) - PAD_FRONT
    valid = jnp.arange(L) >= PAD_FRONT
    past = page_table.shape[1] * cache_k.shape[2]
    pos_s = past + jnp.arange(x_sample.shape[1], dtype=f32)
    xs = x_sample
    kP, vP, lfP, sP, kS, vS, lfS, sS = [], [], [], [], [], [], [], []
    for l in range(DEPTH):
        xp, k1, v1, lf1, s1 = prompt_layer(xp, pos_p, valid, w_in[l], b_f[l], w_out[l], g_pre_mix[l],
                                           g_post_mix[l], g_pre_mlp[l], w_up[l], w_down[l], g_post_mlp[l])
        xs, k2, v2, lf2, s2 = sample_layer(xs, pos_s, cache_k[l], cache_v[l], cache_logf[l], state_ret[l],
                                           page_table, w_in[l], b_f[l], w_out[l], g_pre_mix[l],
                                           g_post_mix[l], g_pre_mlp[l], w_up[l], w_down[l], g_post_mlp[l])
        kP.append(k1); vP.append(v1); lfP.append(lf1); sP.append(s1)
        kS.append(k2); vS.append(v2); lfS.append(lf2); sS.append(s2)
    y_prompt = xp[:, CHUNK:]
    return (y_prompt, xs, jnp.stack(kP), jnp.stack(vP), jnp.stack(lfP), jnp.stack(sP),
            jnp.stack(kS), jnp.stack(vS), jnp.stack(lfS), jnp.stack(sS))
```

```python
import functools

import numpy as np
import jax
import jax.numpy as jnp
from jax import lax
from jax.experimental import pallas as pl
from jax.experimental.pallas import tpu as pltpu

f32 = jnp.float32
bf16 = jnp.bfloat16

D_MODEL = 1024
N_META = 16
PAGE = 128
HD_A = 64
H_A = 8
H_B = 4
DK_B = 64
DV_B = 128
MIX_A = H_A * HD_A
MIX_B = H_B * DV_B
D_FF = 4 * D_MODEL
ROPE_BASE = 10000.0
EPS = 1e-6
NEG = -1e30

QA, KA, VA, QR, KR, VR, GR, FA = 0, 512, 1024, 1536, 1792, 2048, 2560, 3072
D_INP = 3200
LANES = 128

VMEM_LIMIT = 56 * 1024 * 1024


def _cparams(sem):
    return pltpu.CompilerParams(dimension_semantics=sem, vmem_limit_bytes=VMEM_LIMIT)


def _dot(a, b):
    return jnp.dot(a, b, preferred_element_type=f32)


def _dot_nt(a, b):
    return lax.dot_general(a, b, (((1,), (1,)), ((), ())), preferred_element_type=f32)


def _dot_tn(a, b):
    return lax.dot_general(a, b, (((0,), (0,)), ((), ())), preferred_element_type=f32)


def _split3(x):
    hi = x.astype(bf16).astype(f32)
    r = x - hi
    mid = r.astype(bf16).astype(f32)
    lo = (r - mid).astype(bf16).astype(f32)
    return hi, mid, lo


def _rms(x, g):
    return x * lax.rsqrt(jnp.mean(x * x, axis=-1, keepdims=True) + EPS) * g


def _proj_kernel(x_ref, g_ref, w_ref, bf_ref, cos_ref, sin_ref, u_ref,
                 q_ref, k32_ref, v32_ref, kbf_ref, vbf_ref, lf_ref, ct_ref,
                 qr_ref, kr_ref, vr_ref, gr_ref, carry_ref):
    t = pl.program_id(1)
    x = x_ref[0]
    h = _rms(x, g_ref[...]).astype(bf16)
    z = _dot(h, w_ref[...])

    q_ref[0] = (z[:, QA:QA + MIX_A] * (HD_A ** -0.5)).astype(q_ref.dtype)
    k = z[:, KA:KA + MIX_A]
    v = z[:, VA:VA + MIX_A]
    k32_ref[0] = k
    v32_ref[0] = v
    kbf_ref[0] = k.astype(bf16)
    vbf_ref[0] = v.astype(bf16)
    vr_ref[0] = z[:, VR:VR + MIX_B].astype(bf16)
    gr_ref[0] = z[:, GR:GR + MIX_B]

    cos = cos_ref[...]
    sin = sin_ref[...]
    lane = lax.broadcasted_iota(jnp.int32, cos.shape, 1)
    first_half = (lane % DK_B) < (DK_B // 2)
    for s in range(4):
        zs = z[:, QR + s * LANES: QR + (s + 1) * LANES]
        partner = jnp.where(first_half, pltpu.roll(zs, LANES - DK_B // 2, 1),
                            pltpu.roll(zs, DK_B // 2, 1))
        r = zs * cos + partner * sin
        if s < 2:
            qr_ref[0, :, s * LANES:(s + 1) * LANES] = r
        else:
            kr_ref[0, :, (s - 2) * LANES:(s - 1) * LANES] = r * (DK_B ** -0.5)

    fa = z[:, FA:FA + LANES] + bf_ref[...]
    lf = jnp.minimum(fa, 0.0) - jnp.log1p(jnp.exp(-jnp.abs(fa)))
    lf_ref[0] = lf[:, :H_A]
    lft = lf.T[:H_A]
    parts = jnp.concatenate(_split3(lft) + (jnp.zeros_like(lft),), axis=0).astype(bf16)
    cs = _dot(parts, u_ref[...])
    c = cs[0:H_A] + cs[H_A:2 * H_A] + cs[2 * H_A:3 * H_A]

    @pl.when(t == 0)
    def _():
        carry_ref[...] = jnp.zeros_like(carry_ref)

    c = c + carry_ref[:, 0:1]
    ct_ref[0] = c
    carry_ref[...] = jnp.broadcast_to(c[:, -1:], carry_ref.shape)


def _proj(x, g, w, bfp, cos_t, sin_t, u, tm, q_dtype):
    nb, rows, _ = x.shape
    nt = rows // tm
    row_spec = lambda width: pl.BlockSpec((1, tm, width), lambda b, t: (b, t, 0))
    const = lambda shape: pl.BlockSpec(shape, lambda b, t: (0,) * len(shape))
    out_shape = (
        jax.ShapeDtypeStruct((nb, rows, MIX_A), q_dtype),
        jax.ShapeDtypeStruct((nb, rows, MIX_A), f32),
        jax.ShapeDtypeStruct((nb, rows, MIX_A), f32),
        jax.ShapeDtypeStruct((nb, rows, MIX_A), bf16),
        jax.ShapeDtypeStruct((nb, rows, MIX_A), bf16),
        jax.ShapeDtypeStruct((nb, rows, H_A), f32),
        jax.ShapeDtypeStruct((nb, H_A, rows), f32),
        jax.ShapeDtypeStruct((nb, rows, H_B * DK_B), f32),
        jax.ShapeDtypeStruct((nb, rows, H_B * DK_B), f32),
        jax.ShapeDtypeStruct((nb, rows, MIX_B), bf16),
        jax.ShapeDtypeStruct((nb, rows, MIX_B), f32),
    )
    out_specs = (
        row_spec(MIX_A), row_spec(MIX_A), row_spec(MIX_A), row_spec(MIX_A), row_spec(MIX_A),
        row_spec(H_A),
        pl.BlockSpec((1, H_A, tm), lambda b, t: (b, 0, t)),
        row_spec(H_B * DK_B), row_spec(H_B * DK_B), row_spec(MIX_B), row_spec(MIX_B),
    )
    return pl.pallas_call(
        _proj_kernel,
        grid=(nb, nt),
        in_specs=[
            row_spec(D_MODEL),
            const((1, D_MODEL)),
            const((D_MODEL, D_INP)),
            const((1, LANES)),
            pl.BlockSpec((tm, LANES), lambda b, t: (t, 0)),
            pl.BlockSpec((tm, LANES), lambda b, t: (t, 0)),
            const((tm, tm)),
        ],
        out_specs=out_specs,
        out_shape=out_shape,
        scratch_shapes=[pltpu.VMEM((H_A, LANES), f32)],
        compiler_params=_cparams(("arbitrary", "arbitrary")),
        name="proj",
    )(x, g, w, bfp, cos_t, sin_t, u)


def _fox_prompt_kernel(qi_tab, kj_tab, q_ref, k_ref, v_ref, ctk_ref, ctq_ref,
                       km_ref, vm_ref, ctm_ref, o_ref,
                       qm_ref, m_ref, l_ref, acc_ref):
    p_id = pl.program_id(1)
    qi = qi_tab[p_id]
    kj = kj_tab[p_id]
    tq = q_ref.shape[1]
    tk = k_ref.shape[1]
    lane = lax.broadcasted_iota(jnp.int32, (tq, LANES), 1)
    low = lane < HD_A
    cref = ctq_ref[0][:, 0:1]

    @pl.when(kj == 0)
    def _init():
        ctm = ctm_ref[...]
        bias_m = cref - (ctm - ctm[:, N_META - 1:N_META])
        for hp in range(H_A // 2):
            q2 = q_ref[0, :, hp * LANES:(hp + 1) * LANES]
            zero = jnp.zeros_like(q2)
            qm_ref[2 * hp] = jnp.where(low, q2, zero)
            qm_ref[2 * hp + 1] = jnp.where(low, zero, q2)
            km2 = km_ref[0, :, hp * LANES:(hp + 1) * LANES]
            vm2 = vm_ref[0, :, hp * LANES:(hp + 1) * LANES]
            pv = []
            for half in range(2):
                h = 2 * hp + half
                s = _dot_nt(qm_ref[h], km2) + bias_m[h:h + 1, :]
                m0 = jnp.max(s, axis=1, keepdims=True)
                p = jnp.exp(s - m0)
                m_ref[h] = jnp.broadcast_to(m0, (tq, LANES))
                l_ref[h] = jnp.broadcast_to(jnp.sum(p, axis=1, keepdims=True), (tq, LANES))
                pv.append(_dot(p.astype(bf16), vm2))
            acc_ref[hp] = jnp.where(low, pv[0], pv[1])

    bias_all = ctk_ref[0] - cref

    def step(diag):
        if diag:
            row = lax.broadcasted_iota(jnp.int32, (tq, tk), 0)
            col = lax.broadcasted_iota(jnp.int32, (tq, tk), 1)
            keep = col <= row
        for hp in range(H_A // 2):
            k2 = k_ref[0, :, hp * LANES:(hp + 1) * LANES]
            v2 = v_ref[0, :, hp * LANES:(hp + 1) * LANES]
            pv = []
            alpha = []
            for half in range(2):
                h = 2 * hp + half
                s = _dot_nt(qm_ref[h], k2) - bias_all[h:h + 1, :]
                if diag:
                    s = jnp.where(keep, s, NEG)
                m_prev = m_ref[h]
                m_next = jnp.maximum(m_prev, jnp.max(s, axis=1, keepdims=True))
                a = jnp.exp(m_prev - m_next)
                p = jnp.exp(s - pltpu.repeat(m_next, tk // LANES, 1))
                l_ref[h] = a * l_ref[h] + jnp.sum(p, axis=1, keepdims=True)
                m_ref[h] = m_next
                pv.append(_dot(p.astype(bf16), v2))
                alpha.append(a)
            acc_ref[hp] = (acc_ref[hp] * jnp.where(low, alpha[0], alpha[1])
                           + jnp.where(low, pv[0], pv[1]))

    @pl.when(kj < qi)
    def _off():
        step(False)

    @pl.when(kj == qi)
    def _diag():
        step(True)
        for hp in range(H_A // 2):
            l2 = jnp.where(low, l_ref[2 * hp], l_ref[2 * hp + 1])
            o_ref[0, :, hp * LANES:(hp + 1) * LANES] = (acc_ref[hp] / l2).astype(o_ref.dtype)


def _fox_prompt(q, kbf, vbf, ct, km, vm, ctm, meta_blk, tq):
    nb, rows, _ = q.shape
    nq = rows // tq
    qi_np, kj_np = [], []
    for i in range(nq):
        for j in range(i + 1):
            qi_np.append(i)
            kj_np.append(j)
    qi_tab = jnp.asarray(np.array(qi_np, np.int32))
    kj_tab = jnp.asarray(np.array(kj_np, np.int32))
    grid_spec = pltpu.PrefetchScalarGridSpec(
        num_scalar_prefetch=2,
        grid=(nb, len(qi_np)),
        in_specs=[
            pl.BlockSpec((1, tq, MIX_A), lambda b, p, qt, kt: (b, qt[p], 0)),
            pl.BlockSpec((1, tq, MIX_A), lambda b, p, qt, kt: (b, kt[p], 0)),
            pl.BlockSpec((1, tq, MIX_A), lambda b, p, qt, kt: (b, kt[p], 0)),
            pl.BlockSpec((1, H_A, tq), lambda b, p, qt, kt: (b, 0, kt[p])),
            pl.BlockSpec((1, H_A, tq), lambda b, p, qt, kt: (b, 0, qt[p])),
            pl.BlockSpec((1, N_META, MIX_A), lambda b, p, qt, kt: (0, meta_blk, 0)),
            pl.BlockSpec((1, N_META, MIX_A), lambda b, p, qt, kt: (0, meta_blk, 0)),
            pl.BlockSpec((H_A, N_META), lambda b, p, qt, kt: (0, 0)),
        ],
        out_specs=pl.BlockSpec((1, tq, MIX_A), lambda b, p, qt, kt: (b, qt[p], 0)),
        scratch_shapes=[
            pltpu.VMEM((H_A, tq, LANES), bf16),
            pltpu.VMEM((H_A, tq, LANES), f32),
            pltpu.VMEM((H_A, tq, LANES), f32),
            pltpu.VMEM((H_A // 2, tq, LANES), f32),
        ],
    )
    return pl.pallas_call(
        _fox_prompt_kernel,
        grid_spec=grid_spec,
        out_shape=jax.ShapeDtypeStruct((nb, rows, MIX_A), bf16),
        compiler_params=_cparams(("arbitrary", "arbitrary")),
        name="fox_prompt",
    )(qi_tab, kj_tab, q, kbf, vbf, ct, ct, km, vm, ctm)


def _log_gamma():
    return np.log1p(-np.exp2(-5.0 - np.arange(H_B, dtype=np.float64)))


def _ret_tables(c):
    lg = _log_gamma()
    n = np.arange(c, dtype=np.float64)
    rel = n[:, None] - n[None, :]
    dec = np.where(rel >= 0, np.exp(np.maximum(rel, 0.0)[None] * lg[:, None, None]), 0.0)
    q_dec = np.exp((n + 1.0)[None, :] * lg[:, None])
    k_dec = np.exp((c - 1.0 - n)[None, :] * lg[:, None])
    q_dec_full = np.broadcast_to(q_dec[:, :, None], (H_B, c, DV_B))
    k_dec_full = np.repeat(k_dec.T, DK_B, axis=1)
    g_c = [float(v) for v in np.exp(c * lg)]
    return (jnp.asarray(dec, f32), jnp.asarray(q_dec_full, f32), jnp.asarray(k_dec_full, f32), g_c)


def _head_norm_gate(o, gate):
    mu = jnp.mean(o, axis=-1, keepdims=True)
    d = o - mu
    var = jnp.mean(d * d, axis=-1, keepdims=True)
    y = d * lax.rsqrt(var + EPS)
    return y * (gate * jax.nn.sigmoid(gate))


def _ret_prompt_kernel(g_c, qr_ref, kr_ref, vr_ref, gr_ref, dec_ref, qdec_ref, kdec_ref,
                       krm_ref, vrm_ref, kdecm_ref, ob_ref, s_out_ref, s_ref):
    c = pl.program_id(1)

    @pl.when(c == 0)
    def _init():
        kd = (krm_ref[0] * kdecm_ref[...]).astype(bf16)
        for h in range(H_B):
            s_ref[h] = _dot_tn(kd[:, h * DK_B:(h + 1) * DK_B], vrm_ref[0, :, h * DV_B:(h + 1) * DV_B])

    q = qr_ref[0].astype(bf16)
    k = kr_ref[0]
    kb = k.astype(bf16)
    kd = (k * kdec_ref[...]).astype(bf16)
    for h in range(H_B):
        qh = q[:, h * DK_B:(h + 1) * DK_B]
        vh = vr_ref[0, :, h * DV_B:(h + 1) * DV_B]
        s_old = s_ref[h]
        inner = _dot_nt(qh, kb[:, h * DK_B:(h + 1) * DK_B]) * dec_ref[h]
        o = _dot(inner.astype(bf16), vh) + _dot(qh, s_old.astype(bf16)) * qdec_ref[h]
        s_ref[h] = g_c[h] * s_old + _dot_tn(kd[:, h * DK_B:(h + 1) * DK_B], vh)
        gate = gr_ref[0, :, h * DV_B:(h + 1) * DV_B]
        ob_ref[0, :, h * DV_B:(h + 1) * DV_B] = _head_norm_gate(o, gate).astype(ob_ref.dtype)

    @pl.when(c == pl.num_programs(1) - 1)
    def _fin():
        s_out_ref[0] = s_ref[...]


def _ret_prompt(qr, kr, vr, gr, krm_src, vrm_src, meta_blk, chunk):
    nb, rows, _ = qr.shape
    nc = rows // chunk
    dec, qdec, kdec, g_c = _ret_tables(chunk)
    _, _, kdec_m, _ = _ret_tables(N_META)
    row_spec = lambda width: pl.BlockSpec((1, chunk, width), lambda b, c: (b, c, 0))
    const = lambda shape: pl.BlockSpec(shape, lambda b, c: (0,) * len(shape))
    return pl.pallas_call(
        functools.partial(_ret_prompt_kernel, g_c),
        grid=(nb, nc),
        in_specs=[
            row_spec(H_B * DK_B), row_spec(H_B * DK_B), row_spec(MIX_B), row_spec(MIX_B),
            const((H_B, chunk, chunk)), const((H_B, chunk, DV_B)), const((chunk, H_B * DK_B)),
            pl.BlockSpec((1, N_META, H_B * DK_B), lambda b, c: (0, meta_blk, 0)),
            pl.BlockSpec((1, N_META, MIX_B), lambda b, c: (0, meta_blk, 0)),
            const((N_META, H_B * DK_B)),
        ],
        out_specs=(row_spec(MIX_B),
                   pl.BlockSpec((1, H_B, DK_B, DV_B), lambda b, c: (b, 0, 0, 0))),
        out_shape=(jax.ShapeDtypeStruct((nb, rows, MIX_B), bf16),
                   jax.ShapeDtypeStruct((nb, H_B, DK_B, DV_B), f32)),
        scratch_shapes=[pltpu.VMEM((H_B, DK_B, DV_B), f32)],
        compiler_params=_cparams(("arbitrary", "arbitrary")),
        name="ret_prompt",
    )(qr, kr, vr, gr, dec, qdec, kdec, krm_src, vrm_src, kdec_m)


def _ret_sample_kernel(g_c, sb, qr_ref, kr_ref, vr_ref, gr_ref, st_ref, dec_ref, qdec_ref, kdec_ref,
                       ob_ref, s_out_ref):
    t = qr_ref.shape[1] // sb
    q = qr_ref[0].reshape(sb, t, H_B * DK_B)
    k = kr_ref[0].reshape(sb, t, H_B * DK_B)
    kd = k * kdec_ref[...][None]
    v = vr_ref[0].astype(f32).reshape(sb, t, MIX_B)
    gate = gr_ref[0].reshape(sb, t, MIX_B)
    for h in range(H_B):
        qh = q[:, :, h * DK_B:(h + 1) * DK_B]
        kh = k[:, :, h * DK_B:(h + 1) * DK_B]
        kdh = kd[:, :, h * DK_B:(h + 1) * DK_B]
        vh = v[:, :, h * DV_B:(h + 1) * DV_B]
        s_old = st_ref[:, h]
        inner = jnp.einsum('btd,bsd->bts', qh, kh, preferred_element_type=f32) * dec_ref[h][None]
        o = (jnp.einsum('bts,bse->bte', inner, vh, preferred_element_type=f32)
             + jnp.einsum('btd,bde->bte', qh, s_old, preferred_element_type=f32)
             * qdec_ref[h][None])
        s_out_ref[:, h] = g_c[h] * s_old + jnp.einsum('btd,bte->bde', kdh, vh,
                                                      preferred_element_type=f32)
        y = _head_norm_gate(o, gate[:, :, h * DV_B:(h + 1) * DV_B])
        ob_ref[0, :, h * DV_B:(h + 1) * DV_B] = y.reshape(sb * t, DV_B).astype(ob_ref.dtype)


def _ret_sample(qr, kr, vr, gr, state, db, t, sb):
    dec, qdec, kdec, g_c = _ret_tables(t)
    nsteps = db // sb
    row_spec = lambda width: pl.BlockSpec((1, sb * t, width), lambda i: (0, i, 0))
    const = lambda shape: pl.BlockSpec(shape, lambda i: (0,) * len(shape))
    st_spec = pl.BlockSpec((sb, H_B, DK_B, DV_B), lambda i: (i, 0, 0, 0))
    return pl.pallas_call(
        functools.partial(_ret_sample_kernel, g_c, sb),
        grid=(nsteps,),
        in_specs=[row_spec(H_B * DK_B), row_spec(H_B * DK_B), row_spec(MIX_B), row_spec(MIX_B),
                  st_spec, const((H_B, t, t)), const((H_B, t, DV_B)), const((t, H_B * DK_B))],
        out_specs=(pl.BlockSpec((1, sb * t, MIX_B), lambda i: (0, i, 0)), st_spec),
        out_shape=(jax.ShapeDtypeStruct((1, db * t, MIX_B), bf16),
                   jax.ShapeDtypeStruct((db, H_B, DK_B, DV_B), f32)),
        compiler_params=_cparams(("arbitrary",)),
        name="ret_sample",
    )(qr, kr, vr, gr, state, dec, qdec, kdec)


def _mlp_kernel(x_ref, oa_ref, ob_ref, wo_ref, wu_ref, wd_ref, g1_ref, g2_ref, g3_ref, y_ref):
    mixed = _dot(oa_ref[...], wo_ref[0:MIX_A, :]) + _dot(ob_ref[...], wo_ref[MIX_A:, :])
    x1 = x_ref[...] + _rms(mixed, g1_ref[...])
    hn = _rms(x1, g2_ref[...]).astype(bf16)
    u = jnp.square(jnp.maximum(_dot(hn, wu_ref[...]), 0.0)).astype(bf16)
    y_ref[...] = x1 + _rms(_dot(u, wd_ref[...]), g3_ref[...])


def _mlp(x, oa, ob, wo, wu, wd, g1, g2, g3, tm):
    rows = x.shape[0]
    row_spec = lambda width: pl.BlockSpec((tm, width), lambda i: (i, 0))
    const = lambda shape: pl.BlockSpec(shape, lambda i: (0, 0), pipeline_mode=pl.Buffered(1))
    return pl.pallas_call(
        _mlp_kernel,
        grid=(rows // tm,),
        in_specs=[row_spec(D_MODEL), row_spec(MIX_A), row_spec(MIX_B),
                  const((D_MODEL, D_MODEL)), const((D_MODEL, D_FF)), const((D_FF, D_MODEL)),
                  const((1, D_MODEL)), const((1, D_MODEL)), const((1, D_MODEL))],
        out_specs=row_spec(D_MODEL),
        out_shape=jax.ShapeDtypeStruct((rows, D_MODEL), f32),
        compiler_params=_cparams(("arbitrary",)),
        name="merge_mlp",
    )(x, oa, ob, wo, wu, wd, g1, g2, g3)


def _suffix_sums(x):
    lane = lax.broadcasted_iota(jnp.int32, x.shape, 1)
    s = x
    k = 1
    while k < LANES:
        s = s + jnp.where(lane + k < LANES, pltpu.roll(s, LANES - k, 1), 0.0)
        k *= 2
    return s


def _fox_sample_kernel(pg, pt_ref, *refs):
    k_refs = refs[0:pg]
    v_refs = refs[pg:2 * pg]
    b_refs = refs[2 * pg:3 * pg]
    q_ref, kn_ref, vn_ref, cn_ref, o_ref = refs[3 * pg:3 * pg + 5]
    qbd_ref, kcat_ref, vcat_ref, m_ref, l_ref, acc_ref, run_ref = refs[3 * pg + 5:]
    g = pl.program_id(1)
    t = q_ref.shape[1]
    nrow = t * H_A
    hsel = (lax.broadcasted_iota(jnp.int32, (H_A, MIX_A), 1) // HD_A
            == lax.broadcasted_iota(jnp.int32, (H_A, MIX_A), 0))

    @pl.when(g == 0)
    def _init():
        q = q_ref[0]
        qbd = jnp.where(hsel[None], q[:, None, :], 0.0)
        qbd_ref[...] = qbd.reshape(nrow, MIX_A).astype(bf16)
        m_ref[...] = jnp.full_like(m_ref, NEG)
        l_ref[...] = jnp.zeros_like(l_ref)
        acc_ref[...] = jnp.zeros_like(acc_ref)
        run_ref[...] = jnp.zeros_like(run_ref)

    run = run_ref[...]
    biases = []
    for i in range(pg):
        kcat_ref[:, i * PAGE:(i + 1) * PAGE] = k_refs[i][0].reshape(MIX_A, PAGE).astype(bf16)
        vcat_ref[:, i * PAGE:(i + 1) * PAGE] = v_refs[i][0].reshape(MIX_A, PAGE).astype(bf16)
        lf = b_refs[i][0]
        incl = _suffix_sums(lf)
        biases.append(incl - lf + run)
        run = run + incl[:, 0:1]
    run_ref[...] = run
    bias = jnp.concatenate(biases, axis=1)
    width = pg * PAGE
    s = _dot(qbd_ref[...], kcat_ref[...])
    s = (s.reshape(t, H_A, width) + bias[None]).reshape(nrow, width)
    m_prev = m_ref[...]
    m_next = jnp.maximum(m_prev, jnp.max(s, axis=1, keepdims=True))
    a = jnp.exp(m_prev - m_next)
    p = jnp.exp(s - pltpu.repeat(m_next, width // LANES, 1))
    l_ref[...] = a * l_ref[...] + jnp.sum(p, axis=1, keepdims=True)
    m_ref[...] = m_next
    acc_ref[...] = acc_ref[...] * pltpu.repeat(a, MIX_A // LANES, 1) + _dot_nt(p.astype(bf16), vcat_ref[...])

    @pl.when(g == pl.num_programs(1) - 1)
    def _fin():
        pad = jnp.zeros((2 * t - t, MIX_A), f32)
        kn = jnp.concatenate([kn_ref[0], pad], axis=0).astype(bf16)
        vn = jnp.concatenate([vn_ref[0], pad], axis=0).astype(bf16)
        cn = jnp.concatenate([cn_ref[0], jnp.zeros((H_A, t), f32)], axis=1)
        sn = _dot_nt(qbd_ref[...], kn)
        sn = (sn.reshape(t, H_A, 2 * t) - cn[None]).reshape(nrow, 2 * t)
        row_t = lax.broadcasted_iota(jnp.int32, (nrow, 2 * t), 0) // H_A
        col = lax.broadcasted_iota(jnp.int32, (nrow, 2 * t), 1)
        sn = jnp.where(col <= row_t, sn, NEG)
        m_prev = m_ref[...]
        m_next = jnp.maximum(m_prev, jnp.max(sn, axis=1, keepdims=True))
        a = jnp.exp(m_prev - m_next)
        pn = jnp.exp(sn - m_next[:, 0:1])
        l = a * l_ref[...] + jnp.sum(pn, axis=1, keepdims=True)
        acc = acc_ref[...] * pltpu.repeat(a, MIX_A // LANES, 1) + _dot(pn.astype(bf16), vn)
        o = acc / pltpu.repeat(l, MIX_A // LANES, 1)
        o3 = jnp.where(hsel[None], o.reshape(t, H_A, MIX_A), 0.0)
        o_ref[0] = jnp.sum(o3, axis=1).astype(o_ref.dtype)


def _fox_sample(pt_flat, ck, cv, clf, q, kn, vn, cnt, db, t, pg):
    n_pages = pt_flat.shape[0] // db
    ng = n_pages // pg
    nrow = t * H_A

    def page_map(i, ndim):
        return lambda b, g, pt: (pt[(n_pages - 1 - (g * pg + i)) * db + b],) + (0,) * (ndim - 1)

    seq_spec = lambda width: pl.BlockSpec((1, t, width), lambda b, g, pt: (0, b, 0))
    in_specs = ([pl.BlockSpec((1, H_A, HD_A, PAGE), page_map(i, 4)) for i in range(pg)]
                + [pl.BlockSpec((1, H_A, HD_A, PAGE), page_map(i, 4)) for i in range(pg)]
                + [pl.BlockSpec((1, H_A, PAGE), page_map(i, 3)) for i in range(pg)]
                + [seq_spec(MIX_A), seq_spec(MIX_A), seq_spec(MIX_A),
                   pl.BlockSpec((1, H_A, t), lambda b, g, pt: (b, 0, 0))])
    grid_spec = pltpu.PrefetchScalarGridSpec(
        num_scalar_prefetch=1,
        grid=(db, ng),
        in_specs=in_specs,
        out_specs=pl.BlockSpec((1, t, MIX_A), lambda b, g, pt: (b, 0, 0)),
        scratch_shapes=[
            pltpu.VMEM((nrow, MIX_A), bf16),
            pltpu.VMEM((MIX_A, pg * PAGE), bf16),
            pltpu.VMEM((MIX_A, pg * PAGE), bf16),
            pltpu.VMEM((nrow, LANES), f32),
            pltpu.VMEM((nrow, LANES), f32),
            pltpu.VMEM((nrow, MIX_A), f32),
            pltpu.VMEM((H_A, LANES), f32),
        ],
    )
    args = [ck] * pg + [cv] * pg + [clf] * pg + [q, kn, vn, cnt]
    return pl.pallas_call(
        functools.partial(_fox_sample_kernel, pg),
        grid_spec=grid_spec,
        out_shape=jax.ShapeDtypeStruct((db, t, MIX_A), bf16),
        compiler_params=_cparams(("arbitrary", "arbitrary")),
        name="fox_sample",
    )(pt_flat, *args)


def _largest_divisor(n, candidates):
    for c in candidates:
        if n % c == 0:
            return c
    raise ValueError(f"no tile size for {n}")


def _rope_tables(pos):
    half = DK_B // 2
    inv = ROPE_BASE ** (-jnp.arange(half, dtype=f32) / half)
    ang = pos[:, None] * inv[None, :]
    cos = jnp.cos(ang)
    sin = jnp.sin(ang)
    return (jnp.concatenate([cos, cos, cos, cos], axis=1),
            jnp.concatenate([-sin, sin, -sin, sin], axis=1))


def _seg_upper(rows, seg_id):
    i = np.arange(rows)
    u = (i[:, None] <= i[None, :]) & (seg_id[:, None] == seg_id[None, :])
    return jnp.asarray(u.astype(np.float32), bf16)


def kernel(x_prompt, x_sample, cache_k, cache_v, cache_logf, state_ret, page_table, meta_tokens,
           g_pre_mix, w_in, b_f, w_out, g_post_mix, g_pre_mlp, w_up, w_down, g_post_mlp):
    nb, seq, _ = x_prompt.shape
    db, t, _ = x_sample.shape
    n_pool = cache_k.shape[1]
    n_pages = page_table.shape[1]
    assert w_in.shape[0] == 1, "single layer"
    assert seq % 256 == 0 and (db * t) % 16 == 0 and cache_k.shape[2] == PAGE

    w = w_in[0]
    n_fa = 3 * MIX_A
    w_p = jnp.concatenate([w[:, :n_fa], w[:, n_fa + H_A:], w[:, n_fa:n_fa + H_A],
                           jnp.zeros((D_MODEL, D_INP - w.shape[1]), w.dtype)], axis=1).astype(bf16)
    bf_p = jnp.concatenate([b_f[0], jnp.zeros((LANES - H_A,), f32)])[None]
    g0 = g_pre_mix[0][None]
    wo = w_out[0].astype(bf16)
    wu = w_up[0].astype(bf16)
    wd = w_down[0].astype(bf16)
    g1, g2, g3 = g_post_mix[0][None], g_pre_mlp[0][None], g_post_mlp[0][None]

    n_s = db * t
    rows_aux = -(-(n_s + N_META) // LANES) * LANES
    n_pad = rows_aux - n_s - N_META
    x_aux = jnp.concatenate([x_sample.reshape(n_s, D_MODEL), meta_tokens,
                             jnp.zeros((n_pad, D_MODEL), f32)], axis=0)[None]
    past = n_pages * PAGE
    pos_aux = jnp.concatenate([jnp.tile(past + jnp.arange(t, dtype=f32), db),
                               jnp.arange(N_META, dtype=f32), jnp.zeros((n_pad,), f32)])
    cos_a, sin_a = _rope_tables(pos_aux)
    seg_aux = np.concatenate([np.arange(n_s) // t, np.full((N_META,), db), np.full((n_pad,), db + 1)])
    (q_a, k32_a, v32_a, kbf_a, vbf_a, lf_a, ct_a, qr_a, kr_a, vr_a, gr_a) = _proj(
        x_aux, g0, w_p, bf_p, cos_a, sin_a, _seg_upper(rows_aux, seg_aux), rows_aux, f32)
    meta_blk = n_s // N_META

    tm = _largest_divisor(seq, (512, 256))
    cos_p, sin_p = _rope_tables(N_META + jnp.arange(seq, dtype=f32))
    (q_p, k32_p, v32_p, kbf_p, vbf_p, lf_p, ct_p, qr_p, kr_p, vr_p, gr_p) = _proj(
        x_prompt, g0, w_p, bf_p, cos_p, sin_p, _seg_upper(tm, np.zeros((tm,), np.int64)), tm, bf16)

    ct_meta = ct_a[0, :, n_s:n_s + N_META]
    oa_p = _fox_prompt(q_p, kbf_p, vbf_p, ct_p, kbf_a, vbf_a, ct_meta, meta_blk, tm)
    ob_p, s_p = _ret_prompt(qr_p, kr_p, vr_p, gr_p, kr_a, vr_a, meta_blk, 256)
    y_p = _mlp(x_prompt.reshape(nb * seq, D_MODEL), oa_p.reshape(nb * seq, MIX_A),
               ob_p.reshape(nb * seq, MIX_B), wo, wu, wd, g1, g2, g3, tm)

    cnt = ct_a[0, :, :n_s].reshape(H_A, db, t).transpose(1, 0, 2)
    pg = _largest_divisor(n_pages, (8, 4, 2, 1))
    oa_s = _fox_sample(page_table.T.reshape(n_pages * db),
                       cache_k[0].transpose(0, 2, 3, 1), cache_v[0].transpose(0, 2, 3, 1),
                       cache_logf[0].transpose(0, 2, 1),
                       q_a, k32_a, v32_a, cnt, db, t, pg)
    sb = _largest_divisor(db, (8, 4, 2))
    ob_s, s_s = _ret_sample(qr_a, kr_a, vr_a, gr_a, state_ret[0], db, t, sb)
    tms = _largest_divisor(n_s, (512, 256, 128, 64, 32, 16))
    y_s = _mlp(x_sample.reshape(n_s, D_MODEL), oa_s.reshape(n_s, MIX_A), ob_s[0],
               wo, wu, wd, g1, g2, g3, tms)

    def with_meta(meta_rows, tok):
        m = jnp.broadcast_to(meta_rows[None], (nb,) + meta_rows.shape)
        return jnp.concatenate([m, tok], axis=1)

    meta = slice(n_s, n_s + N_META)
    k_prompt = with_meta(k32_a[0, meta], k32_p).reshape(1, nb, N_META + seq, H_A, HD_A)
    v_prompt = with_meta(v32_a[0, meta], v32_p).reshape(1, nb, N_META + seq, H_A, HD_A)
    lf_prompt = with_meta(lf_a[0, meta], lf_p)[None]
    return (y_p.reshape(nb, seq, D_MODEL), y_s.reshape(db, t, D_MODEL),
            k_prompt, v_prompt, lf_prompt, s_p[None],
            k32_a[0, :n_s].reshape(1, db, t, H_A, HD_A), v32_a[0, :n_s].reshape(1, db, t, H_A, HD_A),
            lf_a[0, :n_s].reshape(1, db, t, H_A), s_s[None])
```

```python
import functools

import numpy as np
import jax
import jax.numpy as jnp
from jax import lax
from jax.experimental import pallas as pl
from jax.experimental.pallas import tpu as pltpu

f32 = jnp.float32
bf16 = jnp.bfloat16

D_MODEL = 1024
N_META = 16
PAGE = 128
HD_A = 64
H_A = 8
H_B = 4
DK_B = 64
DV_B = 128
MIX_A = H_A * HD_A
MIX_B = H_B * DV_B
D_FF = 4 * D_MODEL
ROPE_BASE = 10000.0
EPS = 1e-6
NEG = -1e30
LOG2E = 1.4426950408889634

QA, KA, VA, QR, KR, VR, GR, FA = 0, 512, 1024, 1536, 1792, 2048, 2560, 3072
D_INP = 3200
LANES = 128
SUBLANES = 8
AUG = H_A * LANES
ONE_LANE = 3 * H_A
KEY_SUB = 512
SCORES_AHEAD = 2

VMEM_LIMIT = 56 * 1024 * 1024


def _cparams(sem):
    return pltpu.CompilerParams(dimension_semantics=sem, vmem_limit_bytes=VMEM_LIMIT)


def _dot(a, b):
    return jnp.dot(a, b, preferred_element_type=f32)


def _dot_nt(a, b):
    return lax.dot_general(a, b, (((1,), (1,)), ((), ())), preferred_element_type=f32)


def _dot_tn(a, b):
    return lax.dot_general(a, b, (((0,), (0,)), ((), ())), preferred_element_type=f32)


def _split3(x):
    hi = x.astype(bf16).astype(f32)
    r = x - hi
    mid = r.astype(bf16).astype(f32)
    lo = (r - mid).astype(bf16).astype(f32)
    return hi, mid, lo


def _pack3(x, lane):
    hi, mid, lo = _split3(x)
    return (hi + pltpu.roll(mid, H_A, 1) + pltpu.roll(lo, 2 * H_A, 1)
            + jnp.where(lane == ONE_LANE, 1.0, 0.0))


def _rms(x, g):
    return x * lax.rsqrt(jnp.mean(x * x, axis=-1, keepdims=True) + EPS) * g


def _own_lanes(g, lane):
    return (lane < HD_A) if g % 2 == 0 else (lane >= HD_A)


def _placements():
    rk = np.zeros((LANES, AUG), np.float32)
    rq = np.zeros((LANES, AUG), np.float32)
    for g in range(H_A):
        base = g * LANES + (HD_A if g % 2 == 0 else 0)
        for part in range(3):
            rk[part * H_A + g, base + part] = 1.0
            rk[ONE_LANE, base + 3 + part] = 1.0
            rq[ONE_LANE, base + part] = 1.0
            rq[part * H_A + g, base + 3 + part] = 1.0
    return jnp.asarray(rk, bf16), jnp.asarray(rq, bf16)


def _proj_kernel(prompt, x_ref, g_ref, w_ref, bf_ref, cos_ref, sin_ref, a_ref, rk_ref, rq_ref, *refs):
    if prompt:
        (qaug_ref, kaug_ref, vt_ref, k32_ref, v32_ref, lf_ref,
         qr_ref, kr_ref, vr_ref, gr_ref, carry_ref) = refs
    else:
        (q_ref, kaug_ref, vt_ref, k32_ref, v32_ref, lf_ref, ccol_ref,
         qr_ref, kr_ref, vr_ref, gr_ref, carry_ref) = refs
    t = pl.program_id(1)
    x = x_ref[0]
    h = _rms(x, g_ref[...]).astype(bf16)
    z = _dot(h, w_ref[...])
    tm = z.shape[0]
    lane = lax.broadcasted_iota(jnp.int32, (tm, LANES), 1)

    qs = z[:, QA:QA + MIX_A] * (HD_A ** -0.5 * LOG2E)
    k = z[:, KA:KA + MIX_A]
    v = z[:, VA:VA + MIX_A]
    k32_ref[0] = k
    v32_ref[0] = v
    vt_ref[0] = v.T.astype(bf16)
    vr_ref[0] = z[:, VR:VR + MIX_B].astype(bf16)
    gr_ref[0] = z[:, GR:GR + MIX_B]

    cos = cos_ref[...]
    sin = sin_ref[...]
    first_half = (lane % DK_B) < (DK_B // 2)
    for s in range(4):
        zs = z[:, QR + s * LANES: QR + (s + 1) * LANES]
        partner = jnp.where(first_half, pltpu.roll(zs, LANES - DK_B // 2, 1),
                            pltpu.roll(zs, DK_B // 2, 1))
        r = zs * cos + partner * sin
        if s < 2:
            qr_ref[0, :, s * LANES:(s + 1) * LANES] = r
        else:
            kr_ref[0, :, (s - 2) * LANES:(s - 1) * LANES] = r * (DK_B ** -0.5)

    fa = z[:, FA:FA + LANES] + bf_ref[...]
    lf = jnp.where(lane < H_A, jnp.minimum(fa, 0.0) - jnp.log1p(jnp.exp(-jnp.abs(fa))), 0.0)
    lf_ref[0] = lf[:, :H_A]
    hi, mid, lo = _split3(lf)
    parts = (hi + pltpu.roll(mid, H_A, 1) + pltpu.roll(lo, 2 * H_A, 1)).astype(bf16)
    cs = _dot(a_ref[...], parts)
    c = cs + pltpu.roll(cs, LANES - H_A, 1) + pltpu.roll(cs, LANES - 2 * H_A, 1)

    @pl.when(t == 0)
    def _():
        carry_ref[...] = jnp.zeros_like(carry_ref)

    carry = carry_ref[...]
    c = jnp.where(lane < H_A, c + carry[0:1], 0.0)
    carry_ref[...] = jnp.broadcast_to(c[tm - 1:tm], carry_ref.shape)

    e_k = _dot(_pack3(-LOG2E * c, lane).astype(bf16), rk_ref[...])
    for g in range(H_A):
        pair = k[:, (g // 2) * LANES:(g // 2 + 1) * LANES]
        kaug_ref[0, :, g * LANES:(g + 1) * LANES] = jnp.where(
            _own_lanes(g, lane), pair, e_k[:, g * LANES:(g + 1) * LANES]).astype(bf16)
    if prompt:
        lane8 = lane[:SUBLANES]
        qx = _pack3(LOG2E * carry, lane8)
        e_q = _dot(jnp.concatenate([qx, qx], axis=0).astype(bf16), rq_ref[...])
        for g in range(H_A):
            pair = qs[:, (g // 2) * LANES:(g // 2 + 1) * LANES]
            qaug_ref[0, :, g * LANES:(g + 1) * LANES] = jnp.where(
                _own_lanes(g, lane), pair, e_q[0:1, g * LANES:(g + 1) * LANES]).astype(bf16)
    else:
        q_ref[0] = qs
        ccol_ref[0] = c[:, :H_A]


def _proj(prompt, x, g, w, bfp, cos_t, sin_t, a, tm):
    nb, rows, _ = x.shape
    nt = rows // tm
    rk, rq = _placements()
    row_spec = lambda width: pl.BlockSpec((1, tm, width), lambda b, t: (b, t, 0))
    const = lambda shape: pl.BlockSpec(shape, lambda b, t: (0,) * len(shape))
    rows_of = lambda width, dt: jax.ShapeDtypeStruct((nb, rows, width), dt)
    vt_shape = jax.ShapeDtypeStruct((nb, MIX_A, rows), bf16)
    vt_spec = pl.BlockSpec((1, MIX_A, tm), lambda b, t: (b, 0, t))
    tail_shape = (rows_of(H_B * DK_B, f32), rows_of(H_B * DK_B, f32), rows_of(MIX_B, bf16), rows_of(MIX_B, f32))
    tail_spec = (row_spec(H_B * DK_B), row_spec(H_B * DK_B), row_spec(MIX_B), row_spec(MIX_B))
    if prompt:
        out_shape = (rows_of(AUG, bf16), rows_of(AUG, bf16), vt_shape, rows_of(MIX_A, f32), rows_of(MIX_A, f32),
                     rows_of(H_A, f32)) + tail_shape
        out_specs = (row_spec(AUG), row_spec(AUG), vt_spec, row_spec(MIX_A), row_spec(MIX_A),
                     row_spec(H_A)) + tail_spec
    else:
        out_shape = (rows_of(MIX_A, f32), rows_of(AUG, bf16), vt_shape, rows_of(MIX_A, f32), rows_of(MIX_A, f32),
                     rows_of(H_A, f32), rows_of(H_A, f32)) + tail_shape
        out_specs = (row_spec(MIX_A), row_spec(AUG), vt_spec, row_spec(MIX_A), row_spec(MIX_A),
                     row_spec(H_A), row_spec(H_A)) + tail_spec
    return pl.pallas_call(
        functools.partial(_proj_kernel, prompt),
        grid=(nb, nt),
        in_specs=[
            row_spec(D_MODEL),
            const((1, D_MODEL)),
            const((D_MODEL, D_INP)),
            const((1, LANES)),
            pl.BlockSpec((tm, LANES), lambda b, t: (t, 0)),
            pl.BlockSpec((tm, LANES), lambda b, t: (t, 0)),
            const((tm, tm)),
            const((LANES, AUG)),
            const((LANES, AUG)),
        ],
        out_specs=out_specs,
        out_shape=out_shape,
        scratch_shapes=[pltpu.VMEM((SUBLANES, LANES), f32)],
        compiler_params=_cparams(("arbitrary", "arbitrary")),
        name="proj_prompt" if prompt else "proj_aux",
    )(x, g, w, bfp, cos_t, sin_t, a, rk, rq)


def _fox_prompt_kernel(qi_tab, kj_tab, q_ref, k_ref, vt_ref, km_ref, vtm_ref, o_ref,
                       m_ref, l_ref, acc_ref):
    p_id = pl.program_id(1)
    qi = qi_tab[p_id]
    kj = kj_tab[p_id]
    tq = q_ref.shape[1]

    def attend(kk_ref, vv_ref, keep, first):
        nk = kk_ref.shape[1]
        ks = min(nk, KEY_SUB)
        for sub in range(nk // ks):
            keys = slice(sub * ks, (sub + 1) * ks)
            m_new, l_new, scale, pvs = [], [], [], []

            def scores(h):
                cols = slice(h * LANES, (h + 1) * LANES)
                return _dot_nt(kk_ref[0, keys, cols], q_ref[0, :, cols])

            ahead = [scores(h) for h in range(SCORES_AHEAD)]
            for h in range(H_A):
                st = ahead.pop(0)
                if h + SCORES_AHEAD < H_A:
                    ahead.append(scores(h + SCORES_AHEAD))
                if keep is not None:
                    st = jnp.where(keep[keys], st, NEG)
                s3 = st.reshape(ks // SUBLANES, SUBLANES, tq)
                m_cur = jnp.max(jnp.max(s3, axis=0), axis=0, keepdims=True)
                if first and sub == 0:
                    m_next = jnp.broadcast_to(m_cur, (SUBLANES, tq))
                else:
                    m_prev = m_ref[h]
                    m_next = jnp.maximum(m_prev, m_cur)
                p3 = jnp.exp2(s3 - m_next[None])
                p_sum = jnp.sum(p3, axis=0)
                pvs.append(_dot(vv_ref[0, h * HD_A:(h + 1) * HD_A, keys], p3.reshape(ks, tq).astype(bf16)))
                m_new.append(m_next)
                if first and sub == 0:
                    l_new.append(p_sum)
                else:
                    a = jnp.exp2(m_prev - m_next)
                    l_new.append(a * l_ref[h] + p_sum)
                    scale.append(jnp.broadcast_to(a[None], (HD_A // SUBLANES, SUBLANES, tq)).reshape(HD_A, tq))
            m_ref[...] = jnp.stack(m_new)
            l_ref[...] = jnp.stack(l_new)
            pv = jnp.concatenate(pvs, axis=0)
            if first and sub == 0:
                acc_ref[...] = pv
            else:
                acc_ref[...] = acc_ref[...] * jnp.concatenate(scale, axis=0) + pv

    @pl.when(kj == 0)
    def _meta():
        nk = km_ref.shape[1]
        attend(km_ref, vtm_ref, lax.broadcasted_iota(jnp.int32, (nk, tq), 0) < N_META, True)

    @pl.when(kj < qi)
    def _off():
        attend(k_ref, vt_ref, None, False)

    @pl.when(kj == qi)
    def _diag():
        tk = k_ref.shape[1]
        attend(k_ref, vt_ref, (lax.broadcasted_iota(jnp.int32, (tk, tq), 0)
                               <= lax.broadcasted_iota(jnp.int32, (tk, tq), 1)), False)
        for h in range(H_A):
            rows = slice(h * HD_A, (h + 1) * HD_A)
            l_tot = jnp.sum(l_ref[h], axis=0, keepdims=True)
            acc_ref[rows, :] = acc_ref[rows, :] / l_tot
        o_ref[0] = acc_ref[...].T.astype(o_ref.dtype)


def _fox_prompt(q_aug, k_aug, vt, km_src, vtm_src, meta_blk, tq):
    nb, rows, _ = q_aug.shape
    nq = rows // tq
    qi_np, kj_np = [], []
    for i in range(nq):
        for j in range(i + 1):
            qi_np.append(i)
            kj_np.append(j)
    qi_tab = jnp.asarray(np.array(qi_np, np.int32))
    kj_tab = jnp.asarray(np.array(kj_np, np.int32))
    grid_spec = pltpu.PrefetchScalarGridSpec(
        num_scalar_prefetch=2,
        grid=(nb, len(qi_np)),
        in_specs=[
            pl.BlockSpec((1, tq, AUG), lambda b, p, qt, kt: (b, qt[p], 0)),
            pl.BlockSpec((1, tq, AUG), lambda b, p, qt, kt: (b, kt[p], 0)),
            pl.BlockSpec((1, MIX_A, tq), lambda b, p, qt, kt: (b, 0, kt[p])),
            pl.BlockSpec((1, LANES, AUG), lambda b, p, qt, kt: (0, meta_blk, 0)),
            pl.BlockSpec((1, MIX_A, LANES), lambda b, p, qt, kt: (0, 0, meta_blk)),
        ],
        out_specs=pl.BlockSpec((1, tq, MIX_A), lambda b, p, qt, kt: (b, qt[p], 0)),
        scratch_shapes=[
            pltpu.VMEM((H_A, SUBLANES, tq), f32),
            pltpu.VMEM((H_A, SUBLANES, tq), f32),
            pltpu.VMEM((MIX_A, tq), f32),
        ],
    )
    return pl.pallas_call(
        _fox_prompt_kernel,
        grid_spec=grid_spec,
        out_shape=jax.ShapeDtypeStruct((nb, rows, MIX_A), bf16),
        compiler_params=_cparams(("arbitrary", "arbitrary")),
        name="fox_prompt",
    )(qi_tab, kj_tab, q_aug, k_aug, vt, km_src, vtm_src)


def _log_gamma():
    return np.log1p(-np.exp2(-5.0 - np.arange(H_B, dtype=np.float64)))


def _ret_tables(c):
    lg = _log_gamma()
    n = np.arange(c, dtype=np.float64)
    rel = n[:, None] - n[None, :]
    dec = np.where(rel >= 0, np.exp(np.maximum(rel, 0.0)[None] * lg[:, None, None]), 0.0)
    q_dec = np.exp((n + 1.0)[None, :] * lg[:, None])
    k_dec = np.exp((c - 1.0 - n)[None, :] * lg[:, None])
    q_dec_full = np.broadcast_to(q_dec[:, :, None], (H_B, c, DV_B))
    k_dec_full = np.repeat(k_dec.T, DK_B, axis=1)
    g_c = [float(v) for v in np.exp(c * lg)]
    return (jnp.asarray(dec, f32), jnp.asarray(q_dec_full, f32), jnp.asarray(k_dec_full, f32), g_c)


def _head_norm_gate(o, gate):
    mu = jnp.mean(o, axis=-1, keepdims=True)
    d = o - mu
    var = jnp.mean(d * d, axis=-1, keepdims=True)
    y = d * lax.rsqrt(var + EPS)
    return y * (gate * jax.nn.sigmoid(gate))


def _ret_prompt_kernel(g_c, qr_ref, kr_ref, vr_ref, gr_ref, dec_ref, qdec_ref, kdec_ref,
                       krm_ref, vrm_ref, kdecm_ref, ob_ref, s_out_ref, s_ref):
    c = pl.program_id(1)

    @pl.when(c == 0)
    def _init():
        kd = (krm_ref[0] * kdecm_ref[...]).astype(bf16)
        for h in range(H_B):
            s_ref[h] = _dot_tn(kd[:, h * DK_B:(h + 1) * DK_B], vrm_ref[0, :, h * DV_B:(h + 1) * DV_B])

    q = qr_ref[0].astype(bf16)
    k = kr_ref[0]
    kb = k.astype(bf16)
    kd = (k * kdec_ref[...]).astype(bf16)
    heads = range(H_B)
    dk = lambda h: slice(h * DK_B, (h + 1) * DK_B)
    dv = lambda h: slice(h * DV_B, (h + 1) * DV_B)
    s_old = [s_ref[h] for h in heads]
    inner = [_dot_nt(q[:, dk(h)], kb[:, dk(h)]) for h in heads]
    cross = [_dot(q[:, dk(h)], s_old[h].astype(bf16)) for h in heads]
    s_ref[...] = jnp.stack([g_c[h] * s_old[h] + _dot_tn(kd[:, dk(h)], vr_ref[0, :, dv(h)]) for h in heads])
    for h in heads:
        o = _dot((inner[h] * dec_ref[h]).astype(bf16), vr_ref[0, :, dv(h)]) + cross[h] * qdec_ref[h]
        ob_ref[0, :, dv(h)] = _head_norm_gate(o, gr_ref[0, :, dv(h)]).astype(ob_ref.dtype)

    @pl.when(c == pl.num_programs(1) - 1)
    def _fin():
        s_out_ref[0] = s_ref[...]


def _ret_prompt(qr, kr, vr, gr, krm_src, vrm_src, meta_blk, chunk):
    nb, rows, _ = qr.shape
    nc = rows // chunk
    dec, qdec, kdec, g_c = _ret_tables(chunk)
    _, _, kdec_m, _ = _ret_tables(N_META)
    row_spec = lambda width: pl.BlockSpec((1, chunk, width), lambda b, c: (b, c, 0))
    const = lambda shape: pl.BlockSpec(shape, lambda b, c: (0,) * len(shape))
    return pl.pallas_call(
        functools.partial(_ret_prompt_kernel, g_c),
        grid=(nb, nc),
        in_specs=[
            row_spec(H_B * DK_B), row_spec(H_B * DK_B), row_spec(MIX_B), row_spec(MIX_B),
            const((H_B, chunk, chunk)), const((H_B, chunk, DV_B)), const((chunk, H_B * DK_B)),
            pl.BlockSpec((1, N_META, H_B * DK_B), lambda b, c: (0, meta_blk, 0)),
            pl.BlockSpec((1, N_META, MIX_B), lambda b, c: (0, meta_blk, 0)),
            const((N_META, H_B * DK_B)),
        ],
        out_specs=(row_spec(MIX_B),
                   pl.BlockSpec((1, H_B, DK_B, DV_B), lambda b, c: (b, 0, 0, 0))),
        out_shape=(jax.ShapeDtypeStruct((nb, rows, MIX_B), bf16),
                   jax.ShapeDtypeStruct((nb, H_B, DK_B, DV_B), f32)),
        scratch_shapes=[pltpu.VMEM((H_B, DK_B, DV_B), f32)],
        compiler_params=_cparams(("arbitrary", "arbitrary")),
        name="ret_prompt",
    )(qr, kr, vr, gr, dec, qdec, kdec, krm_src, vrm_src, kdec_m)


def _ret_sample_kernel(g_c, sb, qr_ref, kr_ref, vr_ref, gr_ref, st_ref, dec_ref, qdec_ref, kdec_ref,
                       ob_ref, s_out_ref):
    t = qr_ref.shape[1] // sb
    q = qr_ref[0].reshape(sb, t, H_B * DK_B)
    k = kr_ref[0].reshape(sb, t, H_B * DK_B)
    kd = k * kdec_ref[...][None]
    v = vr_ref[0].astype(f32).reshape(sb, t, MIX_B)
    gate = gr_ref[0].reshape(sb, t, MIX_B)
    for h in range(H_B):
        qh = q[:, :, h * DK_B:(h + 1) * DK_B]
        kh = k[:, :, h * DK_B:(h + 1) * DK_B]
        kdh = kd[:, :, h * DK_B:(h + 1) * DK_B]
        vh = v[:, :, h * DV_B:(h + 1) * DV_B]
        s_old = st_ref[:, h]
        inner = jnp.einsum('btd,bsd->bts', qh, kh, preferred_element_type=f32) * dec_ref[h][None]
        o = (jnp.einsum('bts,bse->bte', inner, vh, preferred_element_type=f32)
             + jnp.einsum('btd,bde->bte', qh, s_old, preferred_element_type=f32)
             * qdec_ref[h][None])
        s_out_ref[:, h] = g_c[h] * s_old + jnp.einsum('btd,bte->bde', kdh, vh,
                                                      preferred_element_type=f32)
        y = _head_norm_gate(o, gate[:, :, h * DV_B:(h + 1) * DV_B])
        ob_ref[0, :, h * DV_B:(h + 1) * DV_B] = y.reshape(sb * t, DV_B).astype(ob_ref.dtype)


def _ret_sample(qr, kr, vr, gr, state, db, t, sb):
    dec, qdec, kdec, g_c = _ret_tables(t)
    nsteps = db // sb
    row_spec = lambda width: pl.BlockSpec((1, sb * t, width), lambda i: (0, i, 0))
    const = lambda shape: pl.BlockSpec(shape, lambda i: (0,) * len(shape))
    st_spec = pl.BlockSpec((sb, H_B, DK_B, DV_B), lambda i: (i, 0, 0, 0))
    return pl.pallas_call(
        functools.partial(_ret_sample_kernel, g_c, sb),
        grid=(nsteps,),
        in_specs=[row_spec(H_B * DK_B), row_spec(H_B * DK_B), row_spec(MIX_B), row_spec(MIX_B),
                  st_spec, const((H_B, t, t)), const((H_B, t, DV_B)), const((t, H_B * DK_B))],
        out_specs=(pl.BlockSpec((1, sb * t, MIX_B), lambda i: (0, i, 0)), st_spec),
        out_shape=(jax.ShapeDtypeStruct((1, db * t, MIX_B), bf16),
                   jax.ShapeDtypeStruct((db, H_B, DK_B, DV_B), f32)),
        compiler_params=_cparams(("arbitrary",)),
        name="ret_sample",
    )(qr, kr, vr, gr, state, dec, qdec, kdec)


def _mlp_kernel(x_ref, oa_ref, ob_ref, wo_ref, wu_ref, wd_ref, g1_ref, g2_ref, g3_ref, y_ref):
    mixed = _dot(oa_ref[...], wo_ref[0:MIX_A, :]) + _dot(ob_ref[...], wo_ref[MIX_A:, :])
    x1 = x_ref[...] + _rms(mixed, g1_ref[...])
    hn = _rms(x1, g2_ref[...]).astype(bf16)
    u = jnp.square(jnp.maximum(_dot(hn, wu_ref[...]), 0.0)).astype(bf16)
    y_ref[...] = x1 + _rms(_dot(u, wd_ref[...]), g3_ref[...])


def _mlp(x, oa, ob, wo, wu, wd, g1, g2, g3, tm):
    rows = x.shape[0]
    row_spec = lambda width: pl.BlockSpec((tm, width), lambda i: (i, 0))
    const = lambda shape: pl.BlockSpec(shape, lambda i: (0, 0), pipeline_mode=pl.Buffered(1))
    return pl.pallas_call(
        _mlp_kernel,
        grid=(rows // tm,),
        in_specs=[row_spec(D_MODEL), row_spec(MIX_A), row_spec(MIX_B),
                  const((D_MODEL, D_MODEL)), const((D_MODEL, D_FF)), const((D_FF, D_MODEL)),
                  const((1, D_MODEL)), const((1, D_MODEL)), const((1, D_MODEL))],
        out_specs=row_spec(D_MODEL),
        out_shape=jax.ShapeDtypeStruct((rows, D_MODEL), f32),
        compiler_params=_cparams(("arbitrary",)),
        name="merge_mlp",
    )(x, oa, ob, wo, wu, wd, g1, g2, g3)


def _tile_lanes(x, n):
    return jnp.concatenate([x] * n, axis=1)


def _suffix_sums(x):
    lane = lax.broadcasted_iota(jnp.int32, x.shape, 1)
    s = x
    k = 1
    while k < LANES:
        s = s + jnp.where(lane + k < LANES, pltpu.roll(s, LANES - k, 1), 0.0)
        k *= 2
    return s


def _fox_sample_kernel(pg, n_pages, db, pt_ref, ck_hbm, cv_hbm, clf_hbm, q_ref, kn_ref, vn_ref, cn_ref,
                       o_ref, kbuf, vbuf, lfbuf, ksem, vsem, lfsem,
                       qbd_ref, kcat_ref, vcat_ref, m_ref, l_ref, acc_ref):
    b = pl.program_id(0)
    ng = n_pages // pg
    t = q_ref.shape[1]
    nrow = t * H_A
    width = pg * PAGE

    def page_id(seq, j):
        return pt_ref[(n_pages - 1 - j) * db + seq]

    def kv_copies(pid, slot, i):
        return (pltpu.make_async_copy(ck_hbm.at[pid], kbuf.at[slot, i], ksem.at[slot]),
                pltpu.make_async_copy(cv_hbm.at[pid], vbuf.at[slot, i], vsem.at[slot]))

    def lf_copy(pid, lslot, j):
        return pltpu.make_async_copy(clf_hbm.at[pid], lfbuf.at[lslot, j], lfsem.at[lslot])

    def start_kv(seq, g, slot):
        for i in range(pg):
            for cp in kv_copies(page_id(seq, g * pg + i), slot, i):
                cp.start()

    def wait_kv(slot):
        for i in range(pg):
            for cp in kv_copies(0, slot, i):
                cp.wait()

    def start_lf(seq, lslot):
        for j in range(n_pages):
            lf_copy(page_id(seq, j), lslot, j).start()

    lslot = b % 2

    @pl.when(b == 0)
    def _prime():
        start_lf(0, 0)
        start_kv(0, 0, 0)

    for j in range(n_pages):
        lf_copy(0, lslot, j).wait()

    @pl.when(b + 1 < db)
    def _next_lf():
        start_lf(b + 1, 1 - lslot)

    hsel = (lax.broadcasted_iota(jnp.int32, (H_A, MIX_A), 1) // HD_A
            == lax.broadcasted_iota(jnp.int32, (H_A, MIX_A), 0))
    q = q_ref[0]
    qbd = jnp.where(hsel[None], q[:, None, :], 0.0)
    qbd_ref[...] = qbd.reshape(nrow, MIX_A).astype(bf16)

    run = jnp.zeros((H_A, LANES), f32)
    for g in range(ng):
        slot = g % 2
        if g + 1 < ng:
            start_kv(b, g + 1, 1 - slot)
        else:
            @pl.when(b + 1 < db)
            def _next_seq():
                start_kv(b + 1, 0, 1 - slot)
        wait_kv(slot)
        biases = []
        for i in range(pg):
            kcat_ref[:, i * PAGE:(i + 1) * PAGE] = kbuf[slot, i].reshape(MIX_A, PAGE).astype(bf16)
            vcat_ref[:, i * PAGE:(i + 1) * PAGE] = vbuf[slot, i].reshape(MIX_A, PAGE).astype(bf16)
            lf = lfbuf[lslot, g * pg + i]
            incl = _suffix_sums(lf)
            biases.append(incl - lf + run)
            run = run + incl[:, 0:1]
        bias = jnp.concatenate(biases, axis=1) * LOG2E
        s = _dot(qbd_ref[...], kcat_ref[...])
        s = (s.reshape(t, H_A, width) + bias[None]).reshape(nrow, width)
        m_cur = jnp.max(s, axis=1, keepdims=True)
        if g == 0:
            m_next = jnp.broadcast_to(m_cur, (nrow, LANES))
        else:
            m_prev = m_ref[...]
            m_next = jnp.maximum(m_prev, m_cur)
        p = jnp.exp2(s - _tile_lanes(m_next, width // LANES))
        p_sum = jnp.sum(p, axis=1, keepdims=True)
        pv = _dot_nt(p.astype(bf16), vcat_ref[...])
        if g == 0:
            l_ref[...] = jnp.broadcast_to(p_sum, (nrow, LANES))
            acc_ref[...] = pv
        else:
            a = jnp.exp2(m_prev - m_next)
            l_ref[...] = a * l_ref[...] + p_sum
            acc_ref[...] = acc_ref[...] * _tile_lanes(a, MIX_A // LANES) + pv
        m_ref[...] = m_next

    pad = jnp.zeros((t, MIX_A), f32)
    kn = jnp.concatenate([kn_ref[0], pad], axis=0).astype(bf16)
    vn = jnp.concatenate([vn_ref[0], pad], axis=0).astype(bf16)
    cn = jnp.concatenate([cn_ref[0], jnp.zeros((H_A, t), f32)], axis=1) * LOG2E
    sn = _dot_nt(qbd_ref[...], kn)
    sn = (sn.reshape(t, H_A, 2 * t) - cn[None]).reshape(nrow, 2 * t)
    row_t = lax.broadcasted_iota(jnp.int32, (nrow, 2 * t), 0) // H_A
    col = lax.broadcasted_iota(jnp.int32, (nrow, 2 * t), 1)
    sn = jnp.where(col <= row_t, sn, NEG)
    m_prev = m_ref[...]
    m_next = jnp.maximum(m_prev, jnp.max(sn, axis=1, keepdims=True))
    a = jnp.exp2(m_prev - m_next)
    pn = jnp.exp2(sn - m_next[:, 0:1])
    l = a * l_ref[...] + jnp.sum(pn, axis=1, keepdims=True)
    acc = acc_ref[...] * _tile_lanes(a, MIX_A // LANES) + _dot(pn.astype(bf16), vn)
    o = acc / _tile_lanes(l, MIX_A // LANES)
    o3 = jnp.where(hsel[None], o.reshape(t, H_A, MIX_A), 0.0)
    o_ref[0] = jnp.sum(o3, axis=1).astype(o_ref.dtype)


def _fox_sample(pt_flat, ck, cv, clf, q, kn, vn, cnt, db, t, pg):
    n_pages = pt_flat.shape[0] // db
    assert (n_pages // pg) % 2 == 0, "the two DMA slots alternate within a sequence"
    nrow = t * H_A
    seq_spec = lambda width: pl.BlockSpec((1, t, width), lambda b, pt: (0, b, 0))
    hbm = pl.BlockSpec(memory_space=pl.ANY)
    grid_spec = pltpu.PrefetchScalarGridSpec(
        num_scalar_prefetch=1,
        grid=(db,),
        in_specs=[hbm, hbm, hbm, seq_spec(MIX_A), seq_spec(MIX_A), seq_spec(MIX_A),
                  pl.BlockSpec((1, H_A, t), lambda b, pt: (b, 0, 0))],
        out_specs=pl.BlockSpec((1, t, MIX_A), lambda b, pt: (b, 0, 0)),
        scratch_shapes=[
            pltpu.VMEM((2, pg, H_A, HD_A, PAGE), f32),
            pltpu.VMEM((2, pg, H_A, HD_A, PAGE), f32),
            pltpu.VMEM((2, n_pages, H_A, PAGE), f32),
            pltpu.SemaphoreType.DMA((2,)),
            pltpu.SemaphoreType.DMA((2,)),
            pltpu.SemaphoreType.DMA((2,)),
            pltpu.VMEM((nrow, MIX_A), bf16),
            pltpu.VMEM((MIX_A, pg * PAGE), bf16),
            pltpu.VMEM((MIX_A, pg * PAGE), bf16),
            pltpu.VMEM((nrow, LANES), f32),
            pltpu.VMEM((nrow, LANES), f32),
            pltpu.VMEM((nrow, MIX_A), f32),
        ],
    )
    return pl.pallas_call(
        functools.partial(_fox_sample_kernel, pg, n_pages, db),
        grid_spec=grid_spec,
        out_shape=jax.ShapeDtypeStruct((db, t, MIX_A), bf16),
        compiler_params=_cparams(("arbitrary",)),
        name="fox_sample",
    )(pt_flat, ck, cv, clf, q, kn, vn, cnt)


def _largest_divisor(n, candidates):
    for c in candidates:
        if n % c == 0:
            return c
    raise ValueError(f"no tile size for {n}")


def _rope_tables(pos):
    half = DK_B // 2
    inv = ROPE_BASE ** (-jnp.arange(half, dtype=f32) / half)
    ang = pos[:, None] * inv[None, :]
    cos = jnp.cos(ang)
    sin = jnp.sin(ang)
    return (jnp.concatenate([cos, cos, cos, cos], axis=1),
            jnp.concatenate([-sin, sin, -sin, sin], axis=1))


def _sum_matrix(seg_id, suffix):
    i = np.arange(seg_id.shape[0])
    same = seg_id[:, None] == seg_id[None, :]
    prefix = same & (i[None, :] <= i[:, None]) & ~suffix[:, None]
    later = same & (i[None, :] > i[:, None]) & suffix[:, None]
    return jnp.asarray(prefix.astype(np.float32) - later.astype(np.float32), bf16)


def kernel(x_prompt, x_sample, cache_k, cache_v, cache_logf, state_ret, page_table, meta_tokens,
           g_pre_mix, w_in, b_f, w_out, g_post_mix, g_pre_mlp, w_up, w_down, g_post_mlp):
    nb, seq, _ = x_prompt.shape
    db, t, _ = x_sample.shape
    n_pool = cache_k.shape[1]
    n_pages = page_table.shape[1]
    n_s = db * t
    assert w_in.shape[0] == 1, "single layer"
    assert seq % 256 == 0 and n_s % LANES == 0 and cache_k.shape[2] == PAGE and n_pages % 16 == 0

    w = w_in[0]
    n_fa = 3 * MIX_A
    w_p = jnp.concatenate([w[:, :n_fa], w[:, n_fa + H_A:], w[:, n_fa:n_fa + H_A],
                           jnp.zeros((D_MODEL, D_INP - w.shape[1]), w.dtype)], axis=1).astype(bf16)
    bf_p = jnp.concatenate([b_f[0], jnp.zeros((LANES - H_A,), f32)])[None]
    g0 = g_pre_mix[0][None]
    wo = w_out[0].astype(bf16)
    wu = w_up[0].astype(bf16)
    wd = w_down[0].astype(bf16)
    g1, g2, g3 = g_post_mix[0][None], g_pre_mlp[0][None], g_post_mlp[0][None]

    rows_aux = n_s + LANES
    n_pad = LANES - N_META
    x_aux = jnp.concatenate([x_sample.reshape(n_s, D_MODEL), meta_tokens,
                             jnp.zeros((n_pad, D_MODEL), f32)], axis=0)[None]
    past = n_pages * PAGE
    pos_aux = jnp.concatenate([jnp.tile(past + jnp.arange(t, dtype=f32), db),
                               jnp.arange(N_META, dtype=f32), jnp.zeros((n_pad,), f32)])
    cos_a, sin_a = _rope_tables(pos_aux)
    seg_aux = np.concatenate([np.arange(n_s) // t, np.full((N_META,), db), np.full((n_pad,), db + 1)])
    is_meta = seg_aux == db
    (q_a, kaug_a, vt_a, k32_a, v32_a, lf_a, ccol_a, qr_a, kr_a, vr_a, gr_a) = _proj(
        False, x_aux, g0, w_p, bf_p, cos_a, sin_a, _sum_matrix(seg_aux, is_meta), rows_aux)
    meta_blk16 = n_s // N_META
    meta_blk128 = n_s // LANES

    tm = _largest_divisor(seq, (512, 256))
    cos_p, sin_p = _rope_tables(N_META + jnp.arange(seq, dtype=f32))
    one_seg = np.zeros((tm,), np.int64)
    (qaug_p, kaug_p, vt_p, k32_p, v32_p, lf_p, qr_p, kr_p, vr_p, gr_p) = _proj(
        True, x_prompt, g0, w_p, bf_p, cos_p, sin_p, _sum_matrix(one_seg, one_seg > 0), tm)

    oa_p = _fox_prompt(qaug_p, kaug_p, vt_p, kaug_a, vt_a, meta_blk128, tm)
    ob_p, s_p = _ret_prompt(qr_p, kr_p, vr_p, gr_p, kr_a, vr_a, meta_blk16, 256)
    y_p = _mlp(x_prompt.reshape(nb * seq, D_MODEL), oa_p.reshape(nb * seq, MIX_A),
               ob_p.reshape(nb * seq, MIX_B), wo, wu, wd, g1, g2, g3, tm)

    cnt = ccol_a[0, :n_s].reshape(db, t, H_A).transpose(0, 2, 1)
    pg = 16 if n_pages % 32 == 0 else 8
    oa_s = _fox_sample(page_table.T.reshape(n_pages * db),
                       cache_k[0].transpose(0, 2, 3, 1), cache_v[0].transpose(0, 2, 3, 1),
                       cache_logf[0].transpose(0, 2, 1),
                       q_a, k32_a, v32_a, cnt, db, t, pg)
    sb = _largest_divisor(db, (8, 4, 2))
    ob_s, s_s = _ret_sample(qr_a, kr_a, vr_a, gr_a, state_ret[0], db, t, sb)
    tms = _largest_divisor(n_s, (512, 256, 128))
    y_s = _mlp(x_sample.reshape(n_s, D_MODEL), oa_s.reshape(n_s, MIX_A), ob_s[0],
               wo, wu, wd, g1, g2, g3, tms)

    def with_meta(meta_rows, tok):
        m = jnp.broadcast_to(meta_rows[None], (nb,) + meta_rows.shape)
        return jnp.concatenate([m, tok], axis=1)

    meta = slice(n_s, n_s + N_META)
    k_prompt = with_meta(k32_a[0, meta], k32_p).reshape(1, nb, N_META + seq, H_A, HD_A)
    v_prompt = with_meta(v32_a[0, meta], v32_p).reshape(1, nb, N_META + seq, H_A, HD_A)
    lf_prompt = with_meta(lf_a[0, meta], lf_p)[None]
    return (y_p.reshape(nb, seq, D_MODEL), y_s.reshape(db, t, D_MODEL),
            k_prompt, v_prompt, lf_prompt, s_p[None],
            k32_a[0, :n_s].reshape(1, db, t, H_A, HD_A), v32_a[0, :n_s].reshape(1, db, t, H_A, HD_A),
            lf_a[0, :n_s].reshape(1, db, t, H_A), s_s[None])
```

```python
import functools

import numpy as np
import jax
import jax.numpy as jnp
from jax import lax
from jax.experimental import pallas as pl
from jax.experimental.pallas import tpu as pltpu

f32 = jnp.float32
bf16 = jnp.bfloat16

D_MODEL = 1024
N_META = 16
PAGE = 128
HD_A = 64
H_A = 8
H_B = 4
DK_B = 64
DV_B = 128
MIX_A = H_A * HD_A
MIX_B = H_B * DV_B
D_FF = 4 * D_MODEL
ROPE_BASE = 10000.0
EPS = 1e-6
NEG = -1e30
LOG2E = 1.4426950408889634

QA, KA, VA, QR, KR, VR, GR, FA = 0, 512, 1024, 1536, 1792, 2048, 2560, 3072
D_INP = 3200
LANES = 128
SUBLANES = 8
AUG = H_A * LANES
ONE_LANE = 3 * H_A
KEY_SUB = 512
SCORES_AHEAD = 2

VMEM_LIMIT = 56 * 1024 * 1024


def _cparams(sem):
    return pltpu.CompilerParams(dimension_semantics=sem, vmem_limit_bytes=VMEM_LIMIT)


def _dot(a, b):
    return jnp.dot(a, b, preferred_element_type=f32)


def _dot_nt(a, b):
    return lax.dot_general(a, b, (((1,), (1,)), ((), ())), preferred_element_type=f32)


def _dot_tn(a, b):
    return lax.dot_general(a, b, (((0,), (0,)), ((), ())), preferred_element_type=f32)


def _split3(x):
    hi = x.astype(bf16).astype(f32)
    r = x - hi
    mid = r.astype(bf16).astype(f32)
    lo = (r - mid).astype(bf16).astype(f32)
    return hi, mid, lo


def _pack3(x, lane):
    hi, mid, lo = _split3(x)
    return (hi + pltpu.roll(mid, H_A, 1) + pltpu.roll(lo, 2 * H_A, 1)
            + jnp.where(lane == ONE_LANE, 1.0, 0.0))


def _rms(x, g):
    return x * lax.rsqrt(jnp.mean(x * x, axis=-1, keepdims=True) + EPS) * g


def _own_lanes(g, lane):
    return (lane < HD_A) if g % 2 == 0 else (lane >= HD_A)


def _placements():
    rk = np.zeros((LANES, AUG), np.float32)
    rq = np.zeros((LANES, AUG), np.float32)
    for g in range(H_A):
        base = g * LANES + (HD_A if g % 2 == 0 else 0)
        for part in range(3):
            rk[part * H_A + g, base + part] = 1.0
            rk[ONE_LANE, base + 3 + part] = 1.0
            rq[ONE_LANE, base + part] = 1.0
            rq[part * H_A + g, base + 3 + part] = 1.0
    return jnp.asarray(rk, bf16), jnp.asarray(rq, bf16)


def _proj_kernel(prompt, x_ref, g_ref, w_ref, bf_ref, cos_ref, sin_ref, a_ref, rk_ref, rq_ref, *refs):
    if prompt:
        (qaug_ref, kaug_ref, vt_ref, k32_ref, v32_ref, lf_ref,
         qr_ref, kr_ref, vr_ref, gr_ref, carry_ref) = refs
    else:
        (q_ref, kaug_ref, vt_ref, k32_ref, v32_ref, lf_ref, ccol_ref,
         qr_ref, kr_ref, vr_ref, gr_ref, carry_ref) = refs
    t = pl.program_id(1)
    x = x_ref[0]
    h = _rms(x, g_ref[...]).astype(bf16)
    z = _dot(h, w_ref[...])
    tm = z.shape[0]
    lane = lax.broadcasted_iota(jnp.int32, (tm, LANES), 1)

    qs = z[:, QA:QA + MIX_A] * (HD_A ** -0.5 * LOG2E)
    k = z[:, KA:KA + MIX_A]
    v = z[:, VA:VA + MIX_A]
    k32_ref[0] = k
    v32_ref[0] = v
    vt_ref[0] = v.T.astype(bf16)
    vr_ref[0] = z[:, VR:VR + MIX_B].astype(bf16)
    gr_ref[0] = z[:, GR:GR + MIX_B]

    cos = cos_ref[...]
    sin = sin_ref[...]
    first_half = (lane % DK_B) < (DK_B // 2)
    for s in range(4):
        zs = z[:, QR + s * LANES: QR + (s + 1) * LANES]
        partner = jnp.where(first_half, pltpu.roll(zs, LANES - DK_B // 2, 1),
                            pltpu.roll(zs, DK_B // 2, 1))
        r = zs * cos + partner * sin
        if s < 2:
            qr_ref[0, :, s * LANES:(s + 1) * LANES] = r
        else:
            kr_ref[0, :, (s - 2) * LANES:(s - 1) * LANES] = r * (DK_B ** -0.5)

    fa = z[:, FA:FA + LANES] + bf_ref[...]
    lf = jnp.where(lane < H_A, jnp.minimum(fa, 0.0) - jnp.log1p(jnp.exp(-jnp.abs(fa))), 0.0)
    lf_ref[0] = lf[:, :H_A]
    hi, mid, lo = _split3(lf)
    parts = (hi + pltpu.roll(mid, H_A, 1) + pltpu.roll(lo, 2 * H_A, 1)).astype(bf16)
    cs = _dot(a_ref[...], parts)
    c = cs + pltpu.roll(cs, LANES - H_A, 1) + pltpu.roll(cs, LANES - 2 * H_A, 1)

    @pl.when(t == 0)
    def _():
        carry_ref[...] = jnp.zeros_like(carry_ref)

    carry = carry_ref[...]
    c = jnp.where(lane < H_A, c + carry[0:1], 0.0)
    carry_ref[...] = jnp.broadcast_to(c[tm - 1:tm], carry_ref.shape)

    e_k = _dot(_pack3(-LOG2E * c, lane).astype(bf16), rk_ref[...])
    for g in range(H_A):
        pair = k[:, (g // 2) * LANES:(g // 2 + 1) * LANES]
        kaug_ref[0, :, g * LANES:(g + 1) * LANES] = jnp.where(
            _own_lanes(g, lane), pair, e_k[:, g * LANES:(g + 1) * LANES]).astype(bf16)
    if prompt:
        lane8 = lane[:SUBLANES]
        qx = _pack3(LOG2E * carry, lane8)
        e_q = _dot(jnp.concatenate([qx, qx], axis=0).astype(bf16), rq_ref[...])
        for g in range(H_A):
            pair = qs[:, (g // 2) * LANES:(g // 2 + 1) * LANES]
            qaug_ref[0, :, g * LANES:(g + 1) * LANES] = jnp.where(
                _own_lanes(g, lane), pair, e_q[0:1, g * LANES:(g + 1) * LANES]).astype(bf16)
    else:
        q_ref[0] = qs
        ccol_ref[0] = c[:, :H_A]


def _proj(prompt, x, g, w, bfp, cos_t, sin_t, a, tm):
    nb, rows, _ = x.shape
    nt = rows // tm
    rk, rq = _placements()
    row_spec = lambda width: pl.BlockSpec((1, tm, width), lambda b, t: (b, t, 0))
    const = lambda shape: pl.BlockSpec(shape, lambda b, t: (0,) * len(shape))
    rows_of = lambda width, dt: jax.ShapeDtypeStruct((nb, rows, width), dt)
    vt_shape = jax.ShapeDtypeStruct((nb, MIX_A, rows), bf16)
    vt_spec = pl.BlockSpec((1, MIX_A, tm), lambda b, t: (b, 0, t))
    tail_shape = (rows_of(H_B * DK_B, f32), rows_of(H_B * DK_B, f32), rows_of(MIX_B, bf16), rows_of(MIX_B, f32))
    tail_spec = (row_spec(H_B * DK_B), row_spec(H_B * DK_B), row_spec(MIX_B), row_spec(MIX_B))
    if prompt:
        out_shape = (rows_of(AUG, bf16), rows_of(AUG, bf16), vt_shape, rows_of(MIX_A, f32), rows_of(MIX_A, f32),
                     rows_of(H_A, f32)) + tail_shape
        out_specs = (row_spec(AUG), row_spec(AUG), vt_spec, row_spec(MIX_A), row_spec(MIX_A),
                     row_spec(H_A)) + tail_spec
    else:
        out_shape = (rows_of(MIX_A, f32), rows_of(AUG, bf16), vt_shape, rows_of(MIX_A, f32), rows_of(MIX_A, f32),
                     rows_of(H_A, f32), rows_of(H_A, f32)) + tail_shape
        out_specs = (row_spec(MIX_A), row_spec(AUG), vt_spec, row_spec(MIX_A), row_spec(MIX_A),
                     row_spec(H_A), row_spec(H_A)) + tail_spec
    return pl.pallas_call(
        functools.partial(_proj_kernel, prompt),
        grid=(nb, nt),
        in_specs=[
            row_spec(D_MODEL),
            const((1, D_MODEL)),
            const((D_MODEL, D_INP)),
            const((1, LANES)),
            pl.BlockSpec((tm, LANES), lambda b, t: (t, 0)),
            pl.BlockSpec((tm, LANES), lambda b, t: (t, 0)),
            const((tm, tm)),
            const((LANES, AUG)),
            const((LANES, AUG)),
        ],
        out_specs=out_specs,
        out_shape=out_shape,
        scratch_shapes=[pltpu.VMEM((SUBLANES, LANES), f32)],
        compiler_params=_cparams(("arbitrary", "arbitrary")),
        name="proj_prompt" if prompt else "proj_aux",
    )(x, g, w, bfp, cos_t, sin_t, a, rk, rq)


def _fox_prompt_kernel(qi_tab, kj_tab, q_ref, k_ref, vt_ref, km_ref, vtm_ref, o_ref,
                       m_ref, l_ref, acc_ref):
    p_id = pl.program_id(1)
    qi = qi_tab[p_id]
    kj = kj_tab[p_id]
    tq = q_ref.shape[1]

    def attend(kk_ref, vv_ref, keep, first):
        nk = kk_ref.shape[1]
        ks = min(nk, KEY_SUB)
        for sub in range(nk // ks):
            keys = slice(sub * ks, (sub + 1) * ks)
            m_new, l_new, scale, pvs = [], [], [], []

            def scores(h):
                cols = slice(h * LANES, (h + 1) * LANES)
                return _dot_nt(kk_ref[0, keys, cols], q_ref[0, :, cols])

            ahead = [scores(h) for h in range(SCORES_AHEAD)]
            for h in range(H_A):
                st = ahead.pop(0)
                if h + SCORES_AHEAD < H_A:
                    ahead.append(scores(h + SCORES_AHEAD))
                if keep is not None:
                    st = jnp.where(keep[keys], st, NEG)
                s3 = st.reshape(ks // SUBLANES, SUBLANES, tq)
                m_cur = jnp.max(jnp.max(s3, axis=0), axis=0, keepdims=True)
                if first and sub == 0:
                    m_next = jnp.broadcast_to(m_cur, (SUBLANES, tq))
                else:
                    m_prev = m_ref[h]
                    m_next = jnp.maximum(m_prev, m_cur)
                p3 = jnp.exp2(s3 - m_next[None])
                p_sum = jnp.sum(p3, axis=0)
                pvs.append(_dot(vv_ref[0, h * HD_A:(h + 1) * HD_A, keys], p3.reshape(ks, tq).astype(bf16)))
                m_new.append(m_next)
                if first and sub == 0:
                    l_new.append(p_sum)
                else:
                    a = jnp.exp2(m_prev - m_next)
                    l_new.append(a * l_ref[h] + p_sum)
                    scale.append(jnp.broadcast_to(a[None], (HD_A // SUBLANES, SUBLANES, tq)).reshape(HD_A, tq))
            m_ref[...] = jnp.stack(m_new)
            l_ref[...] = jnp.stack(l_new)
            pv = jnp.concatenate(pvs, axis=0)
            if first and sub == 0:
                acc_ref[...] = pv
            else:
                acc_ref[...] = acc_ref[...] * jnp.concatenate(scale, axis=0) + pv

    @pl.when(kj == 0)
    def _meta():
        nk = km_ref.shape[1]
        attend(km_ref, vtm_ref, lax.broadcasted_iota(jnp.int32, (nk, tq), 0) < N_META, True)

    @pl.when(kj < qi)
    def _off():
        attend(k_ref, vt_ref, None, False)

    @pl.when(kj == qi)
    def _diag():
        tk = k_ref.shape[1]
        attend(k_ref, vt_ref, (lax.broadcasted_iota(jnp.int32, (tk, tq), 0)
                               <= lax.broadcasted_iota(jnp.int32, (tk, tq), 1)), False)
        for h in range(H_A):
            rows = slice(h * HD_A, (h + 1) * HD_A)
            l_tot = jnp.sum(l_ref[h], axis=0, keepdims=True)
            acc_ref[rows, :] = acc_ref[rows, :] / l_tot
        o_ref[0] = acc_ref[...].T.astype(o_ref.dtype)


def _fox_prompt(q_aug, k_aug, vt, km_src, vtm_src, meta_blk, tq):
    nb, rows, _ = q_aug.shape
    nq = rows // tq
    qi_np, kj_np = [], []
    for i in range(nq):
        for j in range(i + 1):
            qi_np.append(i)
            kj_np.append(j)
    qi_tab = jnp.asarray(np.array(qi_np, np.int32))
    kj_tab = jnp.asarray(np.array(kj_np, np.int32))
    grid_spec = pltpu.PrefetchScalarGridSpec(
        num_scalar_prefetch=2,
        grid=(nb, len(qi_np)),
        in_specs=[
            pl.BlockSpec((1, tq, AUG), lambda b, p, qt, kt: (b, qt[p], 0)),
            pl.BlockSpec((1, tq, AUG), lambda b, p, qt, kt: (b, kt[p], 0)),
            pl.BlockSpec((1, MIX_A, tq), lambda b, p, qt, kt: (b, 0, kt[p])),
            pl.BlockSpec((1, LANES, AUG), lambda b, p, qt, kt: (0, meta_blk, 0)),
            pl.BlockSpec((1, MIX_A, LANES), lambda b, p, qt, kt: (0, 0, meta_blk)),
        ],
        out_specs=pl.BlockSpec((1, tq, MIX_A), lambda b, p, qt, kt: (b, qt[p], 0)),
        scratch_shapes=[
            pltpu.VMEM((H_A, SUBLANES, tq), f32),
            pltpu.VMEM((H_A, SUBLANES, tq), f32),
            pltpu.VMEM((MIX_A, tq), f32),
        ],
    )
    return pl.pallas_call(
        _fox_prompt_kernel,
        grid_spec=grid_spec,
        out_shape=jax.ShapeDtypeStruct((nb, rows, MIX_A), bf16),
        compiler_params=_cparams(("arbitrary", "arbitrary")),
        name="fox_prompt",
    )(qi_tab, kj_tab, q_aug, k_aug, vt, km_src, vtm_src)


def _log_gamma():
    return np.log1p(-np.exp2(-5.0 - np.arange(H_B, dtype=np.float64)))


def _ret_tables(c):
    lg = _log_gamma()
    n = np.arange(c, dtype=np.float64)
    rel = n[:, None] - n[None, :]
    dec = np.where(rel >= 0, np.exp(np.maximum(rel, 0.0)[None] * lg[:, None, None]), 0.0)
    q_dec = np.exp((n + 1.0)[None, :] * lg[:, None])
    k_dec = np.exp((c - 1.0 - n)[None, :] * lg[:, None])
    q_dec_full = np.broadcast_to(q_dec[:, :, None], (H_B, c, DV_B))
    k_dec_full = np.repeat(k_dec.T, DK_B, axis=1)
    g_c = [float(v) for v in np.exp(c * lg)]
    return (jnp.asarray(dec, f32), jnp.asarray(q_dec_full, f32), jnp.asarray(k_dec_full, f32), g_c)


def _head_norm_gate(o, gate):
    mu = jnp.mean(o, axis=-1, keepdims=True)
    d = o - mu
    var = jnp.mean(d * d, axis=-1, keepdims=True)
    y = d * lax.rsqrt(var + EPS)
    return y * (gate * jax.nn.sigmoid(gate))


def _ret_prompt_kernel(g_c, qr_ref, kr_ref, vr_ref, gr_ref, dec_ref, qdec_ref, kdec_ref,
                       krm_ref, vrm_ref, kdecm_ref, ob_ref, s_out_ref, s_ref):
    c = pl.program_id(1)

    @pl.when(c == 0)
    def _init():
        kd = (krm_ref[0] * kdecm_ref[...]).astype(bf16)
        for h in range(H_B):
            s_ref[h] = _dot_tn(kd[:, h * DK_B:(h + 1) * DK_B], vrm_ref[0, :, h * DV_B:(h + 1) * DV_B])

    q = qr_ref[0].astype(bf16)
    k = kr_ref[0]
    kb = k.astype(bf16)
    kd = (k * kdec_ref[...]).astype(bf16)
    heads = range(H_B)
    dk = lambda h: slice(h * DK_B, (h + 1) * DK_B)
    dv = lambda h: slice(h * DV_B, (h + 1) * DV_B)
    s_old = [s_ref[h] for h in heads]
    inner = [_dot_nt(q[:, dk(h)], kb[:, dk(h)]) for h in heads]
    cross = [_dot(q[:, dk(h)], s_old[h].astype(bf16)) for h in heads]
    s_ref[...] = jnp.stack([g_c[h] * s_old[h] + _dot_tn(kd[:, dk(h)], vr_ref[0, :, dv(h)]) for h in heads])
    for h in heads:
        o = _dot((inner[h] * dec_ref[h]).astype(bf16), vr_ref[0, :, dv(h)]) + cross[h] * qdec_ref[h]
        ob_ref[0, :, dv(h)] = _head_norm_gate(o, gr_ref[0, :, dv(h)]).astype(ob_ref.dtype)

    @pl.when(c == pl.num_programs(1) - 1)
    def _fin():
        s_out_ref[0] = s_ref[...]


def _ret_prompt(qr, kr, vr, gr, krm_src, vrm_src, meta_blk, chunk):
    nb, rows, _ = qr.shape
    nc = rows // chunk
    dec, qdec, kdec, g_c = _ret_tables(chunk)
    _, _, kdec_m, _ = _ret_tables(N_META)
    row_spec = lambda width: pl.BlockSpec((1, chunk, width), lambda b, c: (b, c, 0))
    const = lambda shape: pl.BlockSpec(shape, lambda b, c: (0,) * len(shape))
    return pl.pallas_call(
        functools.partial(_ret_prompt_kernel, g_c),
        grid=(nb, nc),
        in_specs=[
            row_spec(H_B * DK_B), row_spec(H_B * DK_B), row_spec(MIX_B), row_spec(MIX_B),
            const((H_B, chunk, chunk)), const((H_B, chunk, DV_B)), const((chunk, H_B * DK_B)),
            pl.BlockSpec((1, N_META, H_B * DK_B), lambda b, c: (0, meta_blk, 0)),
            pl.BlockSpec((1, N_META, MIX_B), lambda b, c: (0, meta_blk, 0)),
            const((N_META, H_B * DK_B)),
        ],
        out_specs=(row_spec(MIX_B),
                   pl.BlockSpec((1, H_B, DK_B, DV_B), lambda b, c: (b, 0, 0, 0))),
        out_shape=(jax.ShapeDtypeStruct((nb, rows, MIX_B), bf16),
                   jax.ShapeDtypeStruct((nb, H_B, DK_B, DV_B), f32)),
        scratch_shapes=[pltpu.VMEM((H_B, DK_B, DV_B), f32)],
        compiler_params=_cparams(("arbitrary", "arbitrary")),
        name="ret_prompt",
    )(qr, kr, vr, gr, dec, qdec, kdec, krm_src, vrm_src, kdec_m)


def _ret_sample_kernel(g_c, sb, qr_ref, kr_ref, vr_ref, gr_ref, st_ref, dec_ref, qdec_ref, kdec_ref,
                       ob_ref, s_out_ref):
    t = qr_ref.shape[1] // sb
    q = qr_ref[0].reshape(sb, t, H_B * DK_B)
    k = kr_ref[0].reshape(sb, t, H_B * DK_B)
    kd = k * kdec_ref[...][None]
    v = vr_ref[0].astype(f32).reshape(sb, t, MIX_B)
    gate = gr_ref[0].reshape(sb, t, MIX_B)
    for h in range(H_B):
        qh = q[:, :, h * DK_B:(h + 1) * DK_B]
        kh = k[:, :, h * DK_B:(h + 1) * DK_B]
        kdh = kd[:, :, h * DK_B:(h + 1) * DK_B]
        vh = v[:, :, h * DV_B:(h + 1) * DV_B]
        s_old = st_ref[:, h]
        inner = jnp.einsum('btd,bsd->bts', qh, kh, preferred_element_type=f32) * dec_ref[h][None]
        o = (jnp.einsum('bts,bse->bte', inner, vh, preferred_element_type=f32)
             + jnp.einsum('btd,bde->bte', qh, s_old, preferred_element_type=f32)
             * qdec_ref[h][None])
        s_out_ref[:, h] = g_c[h] * s_old + jnp.einsum('btd,bte->bde', kdh, vh,
                                                      preferred_element_type=f32)
        y = _head_norm_gate(o, gate[:, :, h * DV_B:(h + 1) * DV_B])
        ob_ref[0, :, h * DV_B:(h + 1) * DV_B] = y.reshape(sb * t, DV_B).astype(ob_ref.dtype)


def _ret_sample(qr, kr, vr, gr, state, db, t, sb):
    dec, qdec, kdec, g_c = _ret_tables(t)
    nsteps = db // sb
    row_spec = lambda width: pl.BlockSpec((1, sb * t, width), lambda i: (0, i, 0))
    const = lambda shape: pl.BlockSpec(shape, lambda i: (0,) * len(shape))
    st_spec = pl.BlockSpec((sb, H_B, DK_B, DV_B), lambda i: (i, 0, 0, 0))
    return pl.pallas_call(
        functools.partial(_ret_sample_kernel, g_c, sb),
        grid=(nsteps,),
        in_specs=[row_spec(H_B * DK_B), row_spec(H_B * DK_B), row_spec(MIX_B), row_spec(MIX_B),
                  st_spec, const((H_B, t, t)), const((H_B, t, DV_B)), const((t, H_B * DK_B))],
        out_specs=(pl.BlockSpec((1, sb * t, MIX_B), lambda i: (0, i, 0)), st_spec),
        out_shape=(jax.ShapeDtypeStruct((1, db * t, MIX_B), bf16),
                   jax.ShapeDtypeStruct((db, H_B, DK_B, DV_B), f32)),
        compiler_params=_cparams(("arbitrary",)),
        name="ret_sample",
    )(qr, kr, vr, gr, state, dec, qdec, kdec)


def _mlp_kernel(x_ref, oa_ref, ob_ref, wo_ref, wu_ref, wd_ref, g1_ref, g2_ref, g3_ref, y_ref):
    mixed = _dot(oa_ref[...], wo_ref[0:MIX_A, :]) + _dot(ob_ref[...], wo_ref[MIX_A:, :])
    x1 = x_ref[...] + _rms(mixed, g1_ref[...])
    hn = _rms(x1, g2_ref[...]).astype(bf16)
    u = jnp.square(jnp.maximum(_dot(hn, wu_ref[...]), 0.0)).astype(bf16)
    y_ref[...] = x1 + _rms(_dot(u, wd_ref[...]), g3_ref[...])


def _mlp(x, oa, ob, wo, wu, wd, g1, g2, g3, tm):
    rows = x.shape[0]
    row_spec = lambda width: pl.BlockSpec((tm, width), lambda i: (i, 0))
    const = lambda shape: pl.BlockSpec(shape, lambda i: (0, 0), pipeline_mode=pl.Buffered(1))
    return pl.pallas_call(
        _mlp_kernel,
        grid=(rows // tm,),
        in_specs=[row_spec(D_MODEL), row_spec(MIX_A), row_spec(MIX_B),
                  const((D_MODEL, D_MODEL)), const((D_MODEL, D_FF)), const((D_FF, D_MODEL)),
                  const((1, D_MODEL)), const((1, D_MODEL)), const((1, D_MODEL))],
        out_specs=row_spec(D_MODEL),
        out_shape=jax.ShapeDtypeStruct((rows, D_MODEL), f32),
        compiler_params=_cparams(("arbitrary",)),
        name="merge_mlp",
    )(x, oa, ob, wo, wu, wd, g1, g2, g3)


def _tile_lanes(x, n):
    return jnp.concatenate([x] * n, axis=1)


def _suffix_sums(x):
    lane = lax.broadcasted_iota(jnp.int32, x.shape, 1)
    s = x
    k = 1
    while k < LANES:
        s = s + jnp.where(lane + k < LANES, pltpu.roll(s, LANES - k, 1), 0.0)
        k *= 2
    return s


def _mlp_sample_kernel(pg, n_pages, db, spp, pt_ref,
                       x_ref, oa_ref, ob_ref, wo_ref, wu_ref, wd_ref, g1_ref, g2_ref, g3_ref,
                       ck_hbm, cv_hbm, clf_hbm, q_ref, kn_ref, vn_ref, cn_ref,
                       y_ref, o_ref, kbuf, vbuf, lfbuf, ksem, vsem, lfsem,
                       qbd_ref, kcat_ref, vcat_ref, m_ref, l_ref, acc_ref, hn_ref, u_ref, down_ref):
    ng = n_pages // pg
    t = q_ref.shape[1] // spp
    ff = D_FF // ng

    def mlp_before_scores(g):
        if g == 0:
            mixed = _dot(oa_ref[...], wo_ref[0:MIX_A, :]) + _dot(ob_ref[...], wo_ref[MIX_A:, :])
            x1 = x_ref[...] + _rms(mixed, g1_ref[...])
            y_ref[...] = x1
            hn_ref[...] = _rms(x1, g2_ref[...]).astype(bf16)
        else:
            d = _dot(u_ref[...], wd_ref[(g - 1) * ff:g * ff, :])
            if g == 1:
                down_ref[...] = d
            else:
                down_ref[...] = down_ref[...] + d

    def mlp_before_softmax(g):
        u = _dot(hn_ref[...], wu_ref[:, g * ff:(g + 1) * ff])
        u_ref[...] = jnp.square(jnp.maximum(u, 0.0)).astype(bf16)

    for s in range(spp):
        seq = pl.program_id(0) * spp + s
        o_ref[s] = _sample_attend(pg, n_pages, db, seq, pt_ref, ck_hbm, cv_hbm, clf_hbm,
                                  q_ref[0, s * t:(s + 1) * t], kn_ref[0, s * t:(s + 1) * t],
                                  vn_ref[0, s * t:(s + 1) * t], cn_ref[s],
                                  kbuf, vbuf, lfbuf, ksem, vsem, lfsem,
                                  qbd_ref, kcat_ref, vcat_ref, m_ref, l_ref, acc_ref,
                                  (mlp_before_scores, mlp_before_softmax) if s == 0 else None
                                  ).astype(o_ref.dtype)
    y_ref[...] = y_ref[...] + _rms(down_ref[...], g3_ref[...])


def _sample_attend(pg, n_pages, db, b, pt_ref, ck_hbm, cv_hbm, clf_hbm, q, kn32, vn32, cn8,
                   kbuf, vbuf, lfbuf, ksem, vsem, lfsem,
                   qbd_ref, kcat_ref, vcat_ref, m_ref, l_ref, acc_ref, hooks):
    ng = n_pages // pg
    t = q.shape[0]
    nrow = t * H_A
    width = pg * PAGE

    def page_id(seq, j):
        return pt_ref[(n_pages - 1 - j) * db + seq]

    def kv_copies(pid, slot, i):
        return (pltpu.make_async_copy(ck_hbm.at[pid], kbuf.at[slot, i], ksem.at[slot]),
                pltpu.make_async_copy(cv_hbm.at[pid], vbuf.at[slot, i], vsem.at[slot]))

    def lf_copy(pid, lslot, j):
        return pltpu.make_async_copy(clf_hbm.at[pid], lfbuf.at[lslot, j], lfsem.at[lslot])

    def start_kv(seq, g, slot):
        for i in range(pg):
            for cp in kv_copies(page_id(seq, g * pg + i), slot, i):
                cp.start()

    def wait_kv(slot):
        for i in range(pg):
            for cp in kv_copies(0, slot, i):
                cp.wait()

    def start_lf(seq, lslot):
        for j in range(n_pages):
            lf_copy(page_id(seq, j), lslot, j).start()

    lslot = b % 2

    @pl.when(b == 0)
    def _prime():
        start_lf(0, 0)
        start_kv(0, 0, 0)

    for j in range(n_pages):
        lf_copy(0, lslot, j).wait()

    @pl.when(b + 1 < db)
    def _next_lf():
        start_lf(b + 1, 1 - lslot)

    hsel = (lax.broadcasted_iota(jnp.int32, (H_A, MIX_A), 1) // HD_A
            == lax.broadcasted_iota(jnp.int32, (H_A, MIX_A), 0))
    qbd = jnp.where(hsel[None], q[:, None, :], 0.0)
    qbd_ref[...] = qbd.reshape(nrow, MIX_A).astype(bf16)

    run = jnp.zeros((H_A, LANES), f32)
    for g in range(ng):
        slot = g % 2
        if g + 1 < ng:
            start_kv(b, g + 1, 1 - slot)
        else:
            @pl.when(b + 1 < db)
            def _next_seq():
                start_kv(b + 1, 0, 1 - slot)
        wait_kv(slot)
        if hooks is not None:
            hooks[0](g)
        biases = []
        for i in range(pg):
            kcat_ref[:, i * PAGE:(i + 1) * PAGE] = kbuf[slot, i].reshape(MIX_A, PAGE).astype(bf16)
            vcat_ref[:, i * PAGE:(i + 1) * PAGE] = vbuf[slot, i].reshape(MIX_A, PAGE).astype(bf16)
            lf = lfbuf[lslot, g * pg + i]
            incl = _suffix_sums(lf)
            biases.append(incl - lf + run)
            run = run + incl[:, 0:1]
        bias = jnp.concatenate(biases, axis=1) * LOG2E
        s = _dot(qbd_ref[...], kcat_ref[...])
        if hooks is not None:
            hooks[1](g)
        s = (s.reshape(t, H_A, width) + bias[None]).reshape(nrow, width)
        m_cur = jnp.max(s, axis=1, keepdims=True)
        if g == 0:
            m_next = jnp.broadcast_to(m_cur, (nrow, LANES))
        else:
            m_prev = m_ref[...]
            m_next = jnp.maximum(m_prev, m_cur)
        p = jnp.exp2(s - _tile_lanes(m_next, width // LANES))
        p_sum = jnp.sum(p, axis=1, keepdims=True)
        pv = _dot_nt(p.astype(bf16), vcat_ref[...])
        if g == 0:
            l_ref[...] = jnp.broadcast_to(p_sum, (nrow, LANES))
            acc_ref[...] = pv
        else:
            a = jnp.exp2(m_prev - m_next)
            l_ref[...] = a * l_ref[...] + p_sum
            acc_ref[...] = acc_ref[...] * _tile_lanes(a, MIX_A // LANES) + pv
        m_ref[...] = m_next
    if hooks is not None:
        hooks[0](ng)

    pad = jnp.zeros((t, MIX_A), f32)
    kn = jnp.concatenate([kn32, pad], axis=0).astype(bf16)
    vn = jnp.concatenate([vn32, pad], axis=0).astype(bf16)
    cn = jnp.concatenate([cn8, jnp.zeros((H_A, t), f32)], axis=1) * LOG2E
    sn = _dot_nt(qbd_ref[...], kn)
    sn = (sn.reshape(t, H_A, 2 * t) - cn[None]).reshape(nrow, 2 * t)
    row_t = lax.broadcasted_iota(jnp.int32, (nrow, 2 * t), 0) // H_A
    col = lax.broadcasted_iota(jnp.int32, (nrow, 2 * t), 1)
    sn = jnp.where(col <= row_t, sn, NEG)
    m_prev = m_ref[...]
    m_next = jnp.maximum(m_prev, jnp.max(sn, axis=1, keepdims=True))
    a = jnp.exp2(m_prev - m_next)
    pn = jnp.exp2(sn - m_next[:, 0:1])
    l = a * l_ref[...] + jnp.sum(pn, axis=1, keepdims=True)
    acc = acc_ref[...] * _tile_lanes(a, MIX_A // LANES) + _dot(pn.astype(bf16), vn)
    o = acc / _tile_lanes(l, MIX_A // LANES)
    o3 = jnp.where(hsel[None], o.reshape(t, H_A, MIX_A), 0.0)
    return jnp.sum(o3, axis=1)


def _mlp_sample(x, oa, ob, wo, wu, wd, g1, g2, g3, pt_flat, ck, cv, clf, q, kn, vn, cnt, db, t, pg, tm):
    rows = x.shape[0]
    nsteps = rows // tm
    spp = db // nsteps
    n_pages = pt_flat.shape[0] // db
    ng = n_pages // pg
    assert db % nsteps == 0 and ng % 2 == 0 and D_FF % ng == 0
    nrow = t * H_A
    row_spec = lambda width: pl.BlockSpec((tm, width), lambda i, pt: (i, 0))
    const = lambda shape: pl.BlockSpec(shape, lambda i, pt: (0, 0), pipeline_mode=pl.Buffered(1))
    seq_spec = lambda width: pl.BlockSpec((1, spp * t, width), lambda i, pt: (0, i, 0))
    hbm = pl.BlockSpec(memory_space=pl.ANY)
    grid_spec = pltpu.PrefetchScalarGridSpec(
        num_scalar_prefetch=1,
        grid=(nsteps,),
        in_specs=[row_spec(D_MODEL), row_spec(MIX_A), row_spec(MIX_B),
                  const((D_MODEL, D_MODEL)), const((D_MODEL, D_FF)), const((D_FF, D_MODEL)),
                  const((1, D_MODEL)), const((1, D_MODEL)), const((1, D_MODEL)),
                  hbm, hbm, hbm, seq_spec(MIX_A), seq_spec(MIX_A), seq_spec(MIX_A),
                  pl.BlockSpec((spp, H_A, t), lambda i, pt: (i, 0, 0))],
        out_specs=(row_spec(D_MODEL), pl.BlockSpec((spp, t, MIX_A), lambda i, pt: (i, 0, 0))),
        scratch_shapes=[
            pltpu.VMEM((2, pg, H_A, HD_A, PAGE), f32),
            pltpu.VMEM((2, pg, H_A, HD_A, PAGE), f32),
            pltpu.VMEM((2, n_pages, H_A, PAGE), f32),
            pltpu.SemaphoreType.DMA((2,)),
            pltpu.SemaphoreType.DMA((2,)),
            pltpu.SemaphoreType.DMA((2,)),
            pltpu.VMEM((nrow, MIX_A), bf16),
            pltpu.VMEM((MIX_A, pg * PAGE), bf16),
            pltpu.VMEM((MIX_A, pg * PAGE), bf16),
            pltpu.VMEM((nrow, LANES), f32),
            pltpu.VMEM((nrow, LANES), f32),
            pltpu.VMEM((nrow, MIX_A), f32),
            pltpu.VMEM((tm, D_MODEL), bf16),
            pltpu.VMEM((tm, D_FF // ng), bf16),
            pltpu.VMEM((tm, D_MODEL), f32),
        ],
    )
    return pl.pallas_call(
        functools.partial(_mlp_sample_kernel, pg, n_pages, db, spp),
        grid_spec=grid_spec,
        out_shape=(jax.ShapeDtypeStruct((rows, D_MODEL), f32),
                   jax.ShapeDtypeStruct((db, t, MIX_A), bf16)),
        compiler_params=_cparams(("arbitrary",)),
        name="mlp_sample",
    )(pt_flat, x, oa, ob, wo, wu, wd, g1, g2, g3, ck, cv, clf, q, kn, vn, cnt)


def _largest_divisor(n, candidates):
    for c in candidates:
        if n % c == 0:
            return c
    raise ValueError(f"no tile size for {n}")


def _rope_tables(pos):
    half = DK_B // 2
    inv = ROPE_BASE ** (-jnp.arange(half, dtype=f32) / half)
    ang = pos[:, None] * inv[None, :]
    cos = jnp.cos(ang)
    sin = jnp.sin(ang)
    return (jnp.concatenate([cos, cos, cos, cos], axis=1),
            jnp.concatenate([-sin, sin, -sin, sin], axis=1))


def _sum_matrix(seg_id, suffix):
    i = np.arange(seg_id.shape[0])
    same = seg_id[:, None] == seg_id[None, :]
    prefix = same & (i[None, :] <= i[:, None]) & ~suffix[:, None]
    later = same & (i[None, :] > i[:, None]) & suffix[:, None]
    return jnp.asarray(prefix.astype(np.float32) - later.astype(np.float32), bf16)


def kernel(x_prompt, x_sample, cache_k, cache_v, cache_logf, state_ret, page_table, meta_tokens,
           g_pre_mix, w_in, b_f, w_out, g_post_mix, g_pre_mlp, w_up, w_down, g_post_mlp):
    nb, seq, _ = x_prompt.shape
    db, t, _ = x_sample.shape
    n_pool = cache_k.shape[1]
    n_pages = page_table.shape[1]
    n_s = db * t
    assert w_in.shape[0] == 1, "single layer"
    assert seq % 256 == 0 and n_s % LANES == 0 and cache_k.shape[2] == PAGE and n_pages % 16 == 0

    w = w_in[0]
    n_fa = 3 * MIX_A
    w_p = jnp.concatenate([w[:, :n_fa], w[:, n_fa + H_A:], w[:, n_fa:n_fa + H_A],
                           jnp.zeros((D_MODEL, D_INP - w.shape[1]), w.dtype)], axis=1).astype(bf16)
    bf_p = jnp.concatenate([b_f[0], jnp.zeros((LANES - H_A,), f32)])[None]
    g0 = g_pre_mix[0][None]
    wo = w_out[0].astype(bf16)
    wu = w_up[0].astype(bf16)
    wd = w_down[0].astype(bf16)
    g1, g2, g3 = g_post_mix[0][None], g_pre_mlp[0][None], g_post_mlp[0][None]

    rows_aux = n_s + LANES
    n_pad = LANES - N_META
    x_aux = jnp.concatenate([x_sample.reshape(n_s, D_MODEL), meta_tokens,
                             jnp.zeros((n_pad, D_MODEL), f32)], axis=0)[None]
    past = n_pages * PAGE
    pos_aux = jnp.concatenate([jnp.tile(past + jnp.arange(t, dtype=f32), db),
                               jnp.arange(N_META, dtype=f32), jnp.zeros((n_pad,), f32)])
    cos_a, sin_a = _rope_tables(pos_aux)
    seg_aux = np.concatenate([np.arange(n_s) // t, np.full((N_META,), db), np.full((n_pad,), db + 1)])
    is_meta = seg_aux == db
    (q_a, kaug_a, vt_a, k32_a, v32_a, lf_a, ccol_a, qr_a, kr_a, vr_a, gr_a) = _proj(
        False, x_aux, g0, w_p, bf_p, cos_a, sin_a, _sum_matrix(seg_aux, is_meta), rows_aux)
    meta_blk16 = n_s // N_META
    meta_blk128 = n_s // LANES

    tm = _largest_divisor(seq, (512, 256))
    cos_p, sin_p = _rope_tables(N_META + jnp.arange(seq, dtype=f32))
    one_seg = np.zeros((tm,), np.int64)
    (qaug_p, kaug_p, vt_p, k32_p, v32_p, lf_p, qr_p, kr_p, vr_p, gr_p) = _proj(
        True, x_prompt, g0, w_p, bf_p, cos_p, sin_p, _sum_matrix(one_seg, one_seg > 0), tm)

    oa_p = _fox_prompt(qaug_p, kaug_p, vt_p, kaug_a, vt_a, meta_blk128, tm)
    ob_p, s_p = _ret_prompt(qr_p, kr_p, vr_p, gr_p, kr_a, vr_a, meta_blk16, 256)

    cnt = ccol_a[0, :n_s].reshape(db, t, H_A).transpose(0, 2, 1)
    pg = 16 if n_pages % 32 == 0 else 8
    tm_mlp = 256
    y_p, oa_s = _mlp_sample(x_prompt.reshape(nb * seq, D_MODEL), oa_p.reshape(nb * seq, MIX_A),
                            ob_p.reshape(nb * seq, MIX_B), wo, wu, wd, g1, g2, g3,
                            page_table.T.reshape(n_pages * db),
                            cache_k[0].transpose(0, 2, 3, 1), cache_v[0].transpose(0, 2, 3, 1),
                            cache_logf[0].transpose(0, 2, 1),
                            q_a, k32_a, v32_a, cnt, db, t, pg, tm_mlp)

    sb = _largest_divisor(db, (8, 4, 2))
    ob_s, s_s = _ret_sample(qr_a, kr_a, vr_a, gr_a, state_ret[0], db, t, sb)
    tms = _largest_divisor(n_s, (512, 256, 128))
    y_s = _mlp(x_sample.reshape(n_s, D_MODEL), oa_s.reshape(n_s, MIX_A), ob_s[0],
               wo, wu, wd, g1, g2, g3, tms)

    def with_meta(meta_rows, tok):
        m = jnp.broadcast_to(meta_rows[None], (nb,) + meta_rows.shape)
        return jnp.concatenate([m, tok], axis=1)

    meta = slice(n_s, n_s + N_META)
    k_prompt = with_meta(k32_a[0, meta], k32_p).reshape(1, nb, N_META + seq, H_A, HD_A)
    v_prompt = with_meta(v32_a[0, meta], v32_p).reshape(1, nb, N_META + seq, H_A, HD_A)
    lf_prompt = with_meta(lf_a[0, meta], lf_p)[None]
    return (y_p.reshape(nb, seq, D_MODEL), y_s.reshape(db, t, D_MODEL),
            k_prompt, v_prompt, lf_prompt, s_p[None],
            k32_a[0, :n_s].reshape(1, db, t, H_A, HD_A), v32_a[0, :n_s].reshape(1, db, t, H_A, HD_A),
            lf_a[0, :n_s].reshape(1, db, t, H_A), s_s[None])
```

```python
import functools

import numpy as np
import jax
import jax.numpy as jnp
from jax import lax
from jax.experimental import pallas as pl
from jax.experimental.pallas import tpu as pltpu

f32 = jnp.float32
bf16 = jnp.bfloat16

D_MODEL = 1024
N_META = 16
PAGE = 128
HD_A = 64
H_A = 8
H_B = 4
DK_B = 64
DV_B = 128
MIX_A = H_A * HD_A
MIX_B = H_B * DV_B
D_FF = 4 * D_MODEL
ROPE_BASE = 10000.0
EPS = 1e-6
NEG = -1e30
LOG2E = 1.4426950408889634

QA, KA, VA, QR, KR, VR, GR, FA = 0, 512, 1024, 1536, 1792, 2048, 2560, 3072
D_INP = 3200
LANES = 128
SUBLANES = 8
AUG = H_A * LANES
ONE_LANE = 3 * H_A
VT_ROWS = 80
SCORES_AHEAD = 3

VMEM_LIMIT = 56 * 1024 * 1024


def _cparams(sem):
    return pltpu.CompilerParams(dimension_semantics=sem, vmem_limit_bytes=VMEM_LIMIT)


def _dot(a, b):
    return jnp.dot(a, b, preferred_element_type=f32)


def _dot_nt(a, b):
    return lax.dot_general(a, b, (((1,), (1,)), ((), ())), preferred_element_type=f32)


def _dot_tn(a, b):
    return lax.dot_general(a, b, (((0,), (0,)), ((), ())), preferred_element_type=f32)


def _split3(x):
    hi = x.astype(bf16).astype(f32)
    r = x - hi
    mid = r.astype(bf16).astype(f32)
    lo = (r - mid).astype(bf16).astype(f32)
    return hi, mid, lo


def _pack3(x, lane):
    hi, mid, lo = _split3(x)
    return (hi + pltpu.roll(mid, H_A, 1) + pltpu.roll(lo, 2 * H_A, 1)
            + jnp.where(lane == ONE_LANE, 1.0, 0.0))


def _rms(x, g):
    return x * lax.rsqrt(jnp.mean(x * x, axis=-1, keepdims=True) + EPS) * g


def _own_lanes(g, lane):
    return (lane < HD_A) if g % 2 == 0 else (lane >= HD_A)


def _placements():
    rk = np.zeros((LANES, AUG), np.float32)
    rq = np.zeros((LANES, AUG), np.float32)
    for g in range(H_A):
        base = g * LANES + (HD_A if g % 2 == 0 else 0)
        for part in range(3):
            rk[part * H_A + g, base + part] = 1.0
            rk[ONE_LANE, base + 3 + part] = 1.0
            rq[ONE_LANE, base + part] = 1.0
            rq[part * H_A + g, base + 3 + part] = 1.0
    return jnp.asarray(rk, bf16), jnp.asarray(rq, bf16)


def _proj_kernel(prompt, x_ref, g_ref, w_ref, bf_ref, cos_ref, sin_ref, a_ref, rk_ref, rq_ref, *refs):
    if prompt:
        (km_ref, vm_ref, lfm_ref, qaug_ref, kaug_ref, vt_ref, k_hbm, v_hbm, lf_hbm,
         qr_ref, kr_ref, vr_ref, gr_ref, carry_ref, kst_ref, vst_ref, lst_ref, osem, msem) = refs
    else:
        (q_ref, kaug_ref, vt_ref, k32_ref, v32_ref, lf_ref, ccol_ref,
         qr_ref, kr_ref, vr_ref, gr_ref, carry_ref) = refs
    t = pl.program_id(1)
    x = x_ref[0]
    h = _rms(x, g_ref[...]).astype(bf16)
    z = _dot(h, w_ref[...])
    tm = z.shape[0]
    lane = lax.broadcasted_iota(jnp.int32, (tm, LANES), 1)

    qs = z[:, QA:QA + MIX_A] * (HD_A ** -0.5 * LOG2E)
    k = z[:, KA:KA + MIX_A]
    v = z[:, VA:VA + MIX_A]
    if not prompt:
        k32_ref[0] = k
        v32_ref[0] = v
    vt3 = v.T.reshape(H_A, HD_A, tm)
    extra = jnp.where(lax.broadcasted_iota(jnp.int32, (H_A, VT_ROWS - HD_A, tm), 1) == 0, 1.0, 0.0)
    vt_ref[0] = jnp.concatenate([vt3, extra], axis=1).reshape(H_A * VT_ROWS, tm).astype(bf16)
    vr_ref[0] = z[:, VR:VR + MIX_B].astype(bf16)
    gr_ref[0] = z[:, GR:GR + MIX_B]

    cos = cos_ref[...]
    sin = sin_ref[...]
    first_half = (lane % DK_B) < (DK_B // 2)
    for s in range(4):
        zs = z[:, QR + s * LANES: QR + (s + 1) * LANES]
        partner = jnp.where(first_half, pltpu.roll(zs, LANES - DK_B // 2, 1),
                            pltpu.roll(zs, DK_B // 2, 1))
        r = zs * cos + partner * sin
        if s < 2:
            qr_ref[0, :, s * LANES:(s + 1) * LANES] = r
        else:
            kr_ref[0, :, (s - 2) * LANES:(s - 1) * LANES] = r * (DK_B ** -0.5)

    fa = z[:, FA:FA + LANES] + bf_ref[...]
    lf = jnp.where(lane < H_A, jnp.minimum(fa, 0.0) - jnp.log1p(jnp.exp(-jnp.abs(fa))), 0.0)
    if prompt:
        nt = pl.num_programs(1)
        step = pl.program_id(0) * nt + t
        slot = step % 2

        def out_copies(slot_, b_, t_):
            rows = pl.ds(N_META + t_ * tm, tm)
            return (pltpu.make_async_copy(kst_ref.at[slot_], k_hbm.at[b_, rows], osem.at[0, slot_]),
                    pltpu.make_async_copy(vst_ref.at[slot_], v_hbm.at[b_, rows], osem.at[1, slot_]),
                    pltpu.make_async_copy(lst_ref.at[slot_], lf_hbm.at[b_, rows], osem.at[2, slot_]))

        @pl.when(step >= 2)
        def _reuse():
            for cp in out_copies(slot, 0, 0):
                cp.wait()

        kst_ref[slot] = k
        vst_ref[slot] = v
        lst_ref[slot] = lf[:, :H_A]
        for cp in out_copies(slot, pl.program_id(0), t):
            cp.start()

        @pl.when(t == 0)
        def _meta_rows():
            head = pl.ds(0, N_META)
            cps = (pltpu.make_async_copy(km_ref.at[0], k_hbm.at[pl.program_id(0), head], msem.at[0]),
                   pltpu.make_async_copy(vm_ref.at[0], v_hbm.at[pl.program_id(0), head], msem.at[1]),
                   pltpu.make_async_copy(lfm_ref.at[0], lf_hbm.at[pl.program_id(0), head], msem.at[2]))
            for cp in cps:
                cp.start()
            for cp in cps:
                cp.wait()

        @pl.when(step == pl.num_programs(0) * nt - 1)
        def _drain():
            for cp in out_copies(slot, 0, 0) + out_copies(1 - slot, 0, 0):
                cp.wait()
    else:
        lf_ref[0] = lf[:, :H_A]
    hi, mid, lo = _split3(lf)
    parts = (hi + pltpu.roll(mid, H_A, 1) + pltpu.roll(lo, 2 * H_A, 1)).astype(bf16)
    cs = _dot(a_ref[...], parts)
    c = cs + pltpu.roll(cs, LANES - H_A, 1) + pltpu.roll(cs, LANES - 2 * H_A, 1)

    @pl.when(t == 0)
    def _():
        carry_ref[...] = jnp.zeros_like(carry_ref)

    carry = carry_ref[...]
    c = jnp.where(lane < H_A, c + carry[0:1], 0.0)
    carry_ref[...] = jnp.broadcast_to(c[tm - 1:tm], carry_ref.shape)

    e_k = _dot(_pack3(-LOG2E * c, lane).astype(bf16), rk_ref[...])
    for g in range(H_A):
        pair = k[:, (g // 2) * LANES:(g // 2 + 1) * LANES]
        kaug_ref[0, :, g * LANES:(g + 1) * LANES] = jnp.where(
            _own_lanes(g, lane), pair, e_k[:, g * LANES:(g + 1) * LANES]).astype(bf16)
    if prompt:
        lane8 = lane[:SUBLANES]
        qx = _pack3(LOG2E * carry, lane8)
        e_q = _dot(jnp.concatenate([qx, qx], axis=0).astype(bf16), rq_ref[...])
        for g in range(H_A):
            pair = qs[:, (g // 2) * LANES:(g // 2 + 1) * LANES]
            qaug_ref[0, :, g * LANES:(g + 1) * LANES] = jnp.where(
                _own_lanes(g, lane), pair, e_q[0:1, g * LANES:(g + 1) * LANES]).astype(bf16)
    else:
        q_ref[0] = qs
        ccol_ref[0] = c[:, :H_A]


def _proj(prompt, x, g, w, bfp, cos_t, sin_t, a, tm, meta=None):
    nb, rows, _ = x.shape
    nt = rows // tm
    rk, rq = _placements()
    row_spec = lambda width: pl.BlockSpec((1, tm, width), lambda b, t: (b, t, 0))
    const = lambda shape: pl.BlockSpec(shape, lambda b, t: (0,) * len(shape))
    rows_of = lambda width, dt: jax.ShapeDtypeStruct((nb, rows, width), dt)
    vt_shape = jax.ShapeDtypeStruct((nb, H_A * VT_ROWS, rows), bf16)
    vt_spec = pl.BlockSpec((1, H_A * VT_ROWS, tm), lambda b, t: (b, 0, t))
    tail_shape = (rows_of(H_B * DK_B, f32), rows_of(H_B * DK_B, f32), rows_of(MIX_B, bf16), rows_of(MIX_B, f32))
    tail_spec = (row_spec(H_B * DK_B), row_spec(H_B * DK_B), row_spec(MIX_B), row_spec(MIX_B))
    extra_in, extra_specs = (), []
    scratch = [pltpu.VMEM((SUBLANES, LANES), f32)]
    if prompt:
        assert nb * nt >= 2
        k_a, v_a, lf_a, meta_blk = meta
        full = lambda width: jax.ShapeDtypeStruct((nb, N_META + rows, width), f32)
        hbm = pl.BlockSpec(memory_space=pl.ANY)
        out_shape = (rows_of(AUG, bf16), rows_of(AUG, bf16), vt_shape, full(MIX_A), full(MIX_A),
                     full(H_A)) + tail_shape
        out_specs = (row_spec(AUG), row_spec(AUG), vt_spec, hbm, hbm, hbm) + tail_spec
        meta_spec = lambda width: pl.BlockSpec((1, N_META, width), lambda b, t: (0, meta_blk, 0))
        extra_in = (k_a, v_a, lf_a)
        extra_specs = [meta_spec(MIX_A), meta_spec(MIX_A), meta_spec(H_A)]
        scratch += [pltpu.VMEM((2, tm, MIX_A), f32), pltpu.VMEM((2, tm, MIX_A), f32),
                    pltpu.VMEM((2, tm, H_A), f32),
                    pltpu.SemaphoreType.DMA((3, 2)), pltpu.SemaphoreType.DMA((3,))]
    else:
        out_shape = (rows_of(MIX_A, f32), rows_of(AUG, bf16), vt_shape, rows_of(MIX_A, f32), rows_of(MIX_A, f32),
                     rows_of(H_A, f32), rows_of(H_A, f32)) + tail_shape
        out_specs = (row_spec(MIX_A), row_spec(AUG), vt_spec, row_spec(MIX_A), row_spec(MIX_A),
                     row_spec(H_A), row_spec(H_A)) + tail_spec
    return pl.pallas_call(
        functools.partial(_proj_kernel, prompt),
        grid=(nb, nt),
        in_specs=[
            row_spec(D_MODEL),
            const((1, D_MODEL)),
            const((D_MODEL, D_INP)),
            const((1, LANES)),
            pl.BlockSpec((tm, LANES), lambda b, t: (t, 0)),
            pl.BlockSpec((tm, LANES), lambda b, t: (t, 0)),
            const((tm, tm)),
            const((LANES, AUG)),
            const((LANES, AUG)),
        ] + extra_specs,
        out_specs=out_specs,
        out_shape=out_shape,
        scratch_shapes=scratch,
        compiler_params=_cparams(("arbitrary", "arbitrary")),
        name="proj_prompt" if prompt else "proj_aux",
    )(x, g, w, bfp, cos_t, sin_t, a, rk, rq, *extra_in)


def _fox_prompt_kernel(r, qi_tab, kj_tab, mode_tab, q_ref, k_ref, vt_ref, km_ref, vtm_ref, o_ref,
                       m_ref, acc_ref):
    p_id = pl.program_id(1)
    kj = kj_tab[p_id]
    mode = mode_tab[p_id]
    tq = q_ref.shape[1]

    def attend(kk_ref, vv_ref, keys, keep, first):
        ks = keys.stop - keys.start
        m_new, scale, pvs = [], [], []

        def scores(h):
            cols = slice(h * LANES, (h + 1) * LANES)
            return _dot_nt(kk_ref[0, keys, cols], q_ref[0, :, cols])

        ahead = [scores(h) for h in range(SCORES_AHEAD)]
        for h in range(H_A):
            st = ahead.pop(0)
            if h + SCORES_AHEAD < H_A:
                ahead.append(scores(h + SCORES_AHEAD))
            if keep is not None:
                st = jnp.where(keep, st, NEG)
            s3 = st.reshape(ks // SUBLANES, SUBLANES, tq)
            m_cur = jnp.max(jnp.max(s3, axis=0), axis=0, keepdims=True)
            if first:
                m_next = jnp.broadcast_to(m_cur, (SUBLANES, tq))
            else:
                m_prev = m_ref[h]
                m_next = jnp.maximum(m_prev, m_cur)
                a = jnp.exp2(m_prev - m_next)
                scale.append(jnp.broadcast_to(a[None], (VT_ROWS // SUBLANES, SUBLANES, tq)).reshape(VT_ROWS, tq))
            p = jnp.exp2(s3 - m_next[None]).reshape(ks, tq).astype(bf16)
            pvs.append(_dot(vv_ref[0, h * VT_ROWS:(h + 1) * VT_ROWS, keys], p))
            m_new.append(m_next)
        m_ref[...] = jnp.stack(m_new)
        pv = jnp.concatenate(pvs, axis=0)
        if first:
            acc_ref[...] = pv
        else:
            acc_ref[...] = acc_ref[...] * jnp.concatenate(scale, axis=0) + pv

    block = lambda s: slice(s * tq, (s + 1) * tq)

    @pl.when(kj == 0)
    def _meta():
        nk = km_ref.shape[1]
        attend(km_ref, vtm_ref, slice(0, nk), lax.broadcasted_iota(jnp.int32, (nk, tq), 0) < N_META, True)

    for s in range(r):
        @pl.when((mode == 0) | (mode > s + 1))
        def _before():
            attend(k_ref, vt_ref, block(s), None, False)

    causal = (lax.broadcasted_iota(jnp.int32, (tq, tq), 0) <= lax.broadcasted_iota(jnp.int32, (tq, tq), 1))
    for d in range(1, r + 1):
        @pl.when(mode == d)
        def _last():
            attend(k_ref, vt_ref, block(d - 1), causal, False)
            acc3 = acc_ref[...].reshape(H_A, VT_ROWS, tq)
            o = acc3[:, :HD_A] / acc3[:, HD_A:HD_A + 1]
            o_ref[0] = o.reshape(MIX_A, tq).T.astype(o_ref.dtype)


def _fox_prompt(q_aug, k_aug, vt, km_src, vtm_src, meta_blk, tq):
    nb, rows, _ = q_aug.shape
    nq = rows // tq
    r = 2 if nq % 2 == 0 else 1
    tk = r * tq
    qi_np, kj_np, mode_np = [], [], []
    for i in range(nq):
        for j in range(i // r + 1):
            qi_np.append(i)
            kj_np.append(j)
            mode_np.append(i % r + 1 if j == i // r else 0)
    tabs = [jnp.asarray(np.array(a, np.int32)) for a in (qi_np, kj_np, mode_np)]
    vrows = H_A * VT_ROWS
    grid_spec = pltpu.PrefetchScalarGridSpec(
        num_scalar_prefetch=3,
        grid=(nb, len(qi_np)),
        in_specs=[
            pl.BlockSpec((1, tq, AUG), lambda b, p, qt, kt, mt: (b, qt[p], 0)),
            pl.BlockSpec((1, tk, AUG), lambda b, p, qt, kt, mt: (b, kt[p], 0)),
            pl.BlockSpec((1, vrows, tk), lambda b, p, qt, kt, mt: (b, 0, kt[p])),
            pl.BlockSpec((1, LANES, AUG), lambda b, p, qt, kt, mt: (0, meta_blk, 0)),
            pl.BlockSpec((1, vrows, LANES), lambda b, p, qt, kt, mt: (0, 0, meta_blk)),
        ],
        out_specs=pl.BlockSpec((1, tq, MIX_A), lambda b, p, qt, kt, mt: (b, qt[p], 0)),
        scratch_shapes=[
            pltpu.VMEM((H_A, SUBLANES, tq), f32),
            pltpu.VMEM((vrows, tq), f32),
        ],
    )
    return pl.pallas_call(
        functools.partial(_fox_prompt_kernel, r),
        grid_spec=grid_spec,
        out_shape=jax.ShapeDtypeStruct((nb, rows, MIX_A), bf16),
        compiler_params=_cparams(("arbitrary", "arbitrary")),
        name="fox_prompt",
    )(*tabs, q_aug, k_aug, vt, km_src, vtm_src)


def _log_gamma():
    return np.log1p(-np.exp2(-5.0 - np.arange(H_B, dtype=np.float64)))


def _ret_tables(c):
    lg = _log_gamma()
    n = np.arange(c, dtype=np.float64)
    rel = n[:, None] - n[None, :]
    dec = np.where(rel >= 0, np.exp(np.maximum(rel, 0.0)[None] * lg[:, None, None]), 0.0)
    q_dec = np.exp((n + 1.0)[None, :] * lg[:, None])
    k_dec = np.exp((c - 1.0 - n)[None, :] * lg[:, None])
    q_dec_full = np.broadcast_to(q_dec[:, :, None], (H_B, c, DV_B))
    k_dec_full = np.repeat(k_dec.T, DK_B, axis=1)
    g_c = [float(v) for v in np.exp(c * lg)]
    return (jnp.asarray(dec, f32), jnp.asarray(q_dec_full, f32), jnp.asarray(k_dec_full, f32), g_c)


def _head_norm_gate(o, gate):
    mu = jnp.mean(o, axis=-1, keepdims=True)
    d = o - mu
    var = jnp.mean(d * d, axis=-1, keepdims=True)
    y = d * lax.rsqrt(var + EPS)
    return y * (gate * jax.nn.sigmoid(gate))


def _ret_prompt_kernel(g_c, qr_ref, kr_ref, vr_ref, gr_ref, dec_ref, qdec_ref, kdec_ref,
                       krm_ref, vrm_ref, kdecm_ref, ob_ref, s_out_ref, s_ref):
    c = pl.program_id(1)

    @pl.when(c == 0)
    def _init():
        kd = (krm_ref[0] * kdecm_ref[...]).astype(bf16)
        for h in range(H_B):
            s_ref[h] = _dot_tn(kd[:, h * DK_B:(h + 1) * DK_B], vrm_ref[0, :, h * DV_B:(h + 1) * DV_B])

    q = qr_ref[0].astype(bf16)
    k = kr_ref[0]
    kb = k.astype(bf16)
    kd = (k * kdec_ref[...]).astype(bf16)
    heads = range(H_B)
    dk = lambda h: slice(h * DK_B, (h + 1) * DK_B)
    dv = lambda h: slice(h * DV_B, (h + 1) * DV_B)
    s_old = [s_ref[h] for h in heads]
    inner = [_dot_nt(q[:, dk(h)], kb[:, dk(h)]) for h in heads]
    cross = [_dot(q[:, dk(h)], s_old[h].astype(bf16)) for h in heads]
    s_ref[...] = jnp.stack([g_c[h] * s_old[h] + _dot_tn(kd[:, dk(h)], vr_ref[0, :, dv(h)]) for h in heads])
    for h in heads:
        o = _dot((inner[h] * dec_ref[h]).astype(bf16), vr_ref[0, :, dv(h)]) + cross[h] * qdec_ref[h]
        ob_ref[0, :, dv(h)] = _head_norm_gate(o, gr_ref[0, :, dv(h)]).astype(ob_ref.dtype)

    @pl.when(c == pl.num_programs(1) - 1)
    def _fin():
        s_out_ref[0] = s_ref[...]


def _ret_prompt(qr, kr, vr, gr, krm_src, vrm_src, meta_blk, chunk):
    nb, rows, _ = qr.shape
    nc = rows // chunk
    dec, qdec, kdec, g_c = _ret_tables(chunk)
    _, _, kdec_m, _ = _ret_tables(N_META)
    row_spec = lambda width: pl.BlockSpec((1, chunk, width), lambda b, c: (b, c, 0))
    const = lambda shape: pl.BlockSpec(shape, lambda b, c: (0,) * len(shape))
    return pl.pallas_call(
        functools.partial(_ret_prompt_kernel, g_c),
        grid=(nb, nc),
        in_specs=[
            row_spec(H_B * DK_B), row_spec(H_B * DK_B), row_spec(MIX_B), row_spec(MIX_B),
            const((H_B, chunk, chunk)), const((H_B, chunk, DV_B)), const((chunk, H_B * DK_B)),
            pl.BlockSpec((1, N_META, H_B * DK_B), lambda b, c: (0, meta_blk, 0)),
            pl.BlockSpec((1, N_META, MIX_B), lambda b, c: (0, meta_blk, 0)),
            const((N_META, H_B * DK_B)),
        ],
        out_specs=(row_spec(MIX_B),
                   pl.BlockSpec((1, H_B, DK_B, DV_B), lambda b, c: (b, 0, 0, 0))),
        out_shape=(jax.ShapeDtypeStruct((nb, rows, MIX_B), bf16),
                   jax.ShapeDtypeStruct((nb, H_B, DK_B, DV_B), f32)),
        scratch_shapes=[pltpu.VMEM((H_B, DK_B, DV_B), f32)],
        compiler_params=_cparams(("arbitrary", "arbitrary")),
        name="ret_prompt",
    )(qr, kr, vr, gr, dec, qdec, kdec, krm_src, vrm_src, kdec_m)


def _ret_sample_kernel(g_c, sb, qr_ref, kr_ref, vr_ref, gr_ref, st_ref, dec_ref, qdec_ref, kdec_ref,
                       ob_ref, s_out_ref):
    t = qr_ref.shape[1] // sb
    q = qr_ref[0].reshape(sb, t, H_B * DK_B)
    k = kr_ref[0].reshape(sb, t, H_B * DK_B)
    kd = k * kdec_ref[...][None]
    v = vr_ref[0].astype(f32).reshape(sb, t, MIX_B)
    gate = gr_ref[0].reshape(sb, t, MIX_B)
    for h in range(H_B):
        qh = q[:, :, h * DK_B:(h + 1) * DK_B]
        kh = k[:, :, h * DK_B:(h + 1) * DK_B]
        kdh = kd[:, :, h * DK_B:(h + 1) * DK_B]
        vh = v[:, :, h * DV_B:(h + 1) * DV_B]
        s_old = st_ref[:, h]
        inner = jnp.einsum('btd,bsd->bts', qh, kh, preferred_element_type=f32) * dec_ref[h][None]
        o = (jnp.einsum('bts,bse->bte', inner, vh, preferred_element_type=f32)
             + jnp.einsum('btd,bde->bte', qh, s_old, preferred_element_type=f32)
             * qdec_ref[h][None])
        s_out_ref[:, h] = g_c[h] * s_old + jnp.einsum('btd,bte->bde', kdh, vh,
                                                      preferred_element_type=f32)
        y = _head_norm_gate(o, gate[:, :, h * DV_B:(h + 1) * DV_B])
        ob_ref[0, :, h * DV_B:(h + 1) * DV_B] = y.reshape(sb * t, DV_B).astype(ob_ref.dtype)


def _ret_sample(qr, kr, vr, gr, state, db, t, sb):
    dec, qdec, kdec, g_c = _ret_tables(t)
    nsteps = db // sb
    row_spec = lambda width: pl.BlockSpec((1, sb * t, width), lambda i: (0, i, 0))
    const = lambda shape: pl.BlockSpec(shape, lambda i: (0,) * len(shape))
    st_spec = pl.BlockSpec((sb, H_B, DK_B, DV_B), lambda i: (i, 0, 0, 0))
    return pl.pallas_call(
        functools.partial(_ret_sample_kernel, g_c, sb),
        grid=(nsteps,),
        in_specs=[row_spec(H_B * DK_B), row_spec(H_B * DK_B), row_spec(MIX_B), row_spec(MIX_B),
                  st_spec, const((H_B, t, t)), const((H_B, t, DV_B)), const((t, H_B * DK_B))],
        out_specs=(pl.BlockSpec((1, sb * t, MIX_B), lambda i: (0, i, 0)), st_spec),
        out_shape=(jax.ShapeDtypeStruct((1, db * t, MIX_B), bf16),
                   jax.ShapeDtypeStruct((db, H_B, DK_B, DV_B), f32)),
        compiler_params=_cparams(("arbitrary",)),
        name="ret_sample",
    )(qr, kr, vr, gr, state, dec, qdec, kdec)


def _mlp_kernel(x_ref, oa_ref, ob_ref, wo_ref, wu_ref, wd_ref, g1_ref, g2_ref, g3_ref, y_ref):
    mixed = _dot(oa_ref[...], wo_ref[0:MIX_A, :]) + _dot(ob_ref[...], wo_ref[MIX_A:, :])
    x1 = x_ref[...] + _rms(mixed, g1_ref[...])
    hn = _rms(x1, g2_ref[...]).astype(bf16)
    u = jnp.square(jnp.maximum(_dot(hn, wu_ref[...]), 0.0)).astype(bf16)
    y_ref[...] = x1 + _rms(_dot(u, wd_ref[...]), g3_ref[...])


def _mlp(x, oa, ob, wo, wu, wd, g1, g2, g3, tm):
    rows = x.shape[0]
    row_spec = lambda width: pl.BlockSpec((tm, width), lambda i: (i, 0))
    const = lambda shape: pl.BlockSpec(shape, lambda i: (0, 0), pipeline_mode=pl.Buffered(1))
    return pl.pallas_call(
        _mlp_kernel,
        grid=(rows // tm,),
        in_specs=[row_spec(D_MODEL), row_spec(MIX_A), row_spec(MIX_B),
                  const((D_MODEL, D_MODEL)), const((D_MODEL, D_FF)), const((D_FF, D_MODEL)),
                  const((1, D_MODEL)), const((1, D_MODEL)), const((1, D_MODEL))],
        out_specs=row_spec(D_MODEL),
        out_shape=jax.ShapeDtypeStruct((rows, D_MODEL), f32),
        compiler_params=_cparams(("arbitrary",)),
        name="merge_mlp",
    )(x, oa, ob, wo, wu, wd, g1, g2, g3)


def _tile_lanes(x, n):
    return jnp.concatenate([x] * n, axis=1)


def _suffix_sums(x):
    lane = lax.broadcasted_iota(jnp.int32, x.shape, 1)
    s = x
    k = 1
    while k < LANES:
        s = s + jnp.where(lane + k < LANES, pltpu.roll(s, LANES - k, 1), 0.0)
        k *= 2
    return s


def _mlp_sample_kernel(pg, n_pages, db, spp, pt_ref,
                       x_ref, oa_ref, ob_ref, wo_ref, wu_ref, wd_ref, g1_ref, g2_ref, g3_ref,
                       ck_hbm, cv_hbm, clf_hbm, q_ref, kn_ref, vn_ref, cn_ref,
                       y_ref, o_ref, kbuf, vbuf, lfbuf, ksem, vsem, lfsem,
                       qbd_ref, kcat_ref, vcat_ref, m_ref, l_ref, acc_ref, hn_ref, u_ref, down_ref):
    ng = n_pages // pg
    t = q_ref.shape[1] // spp
    ff = D_FF // ng

    def mlp_before_scores(g):
        if g == 0:
            mixed = _dot(oa_ref[...], wo_ref[0:MIX_A, :]) + _dot(ob_ref[...], wo_ref[MIX_A:, :])
            x1 = x_ref[...] + _rms(mixed, g1_ref[...])
            y_ref[...] = x1
            hn_ref[...] = _rms(x1, g2_ref[...]).astype(bf16)
        else:
            d = _dot(u_ref[...], wd_ref[(g - 1) * ff:g * ff, :])
            if g == 1:
                down_ref[...] = d
            else:
                down_ref[...] = down_ref[...] + d

    def mlp_before_softmax(g):
        u = _dot(hn_ref[...], wu_ref[:, g * ff:(g + 1) * ff])
        u_ref[...] = jnp.square(jnp.maximum(u, 0.0)).astype(bf16)

    for s in range(spp):
        seq = pl.program_id(0) * spp + s
        o_ref[s] = _sample_attend(pg, n_pages, db, seq, pt_ref, ck_hbm, cv_hbm, clf_hbm,
                                  q_ref[0, s * t:(s + 1) * t], kn_ref[0, s * t:(s + 1) * t],
                                  vn_ref[0, s * t:(s + 1) * t], cn_ref[s],
                                  kbuf, vbuf, lfbuf, ksem, vsem, lfsem,
                                  qbd_ref, kcat_ref, vcat_ref, m_ref, l_ref, acc_ref,
                                  (mlp_before_scores, mlp_before_softmax) if s == 0 else None
                                  ).astype(o_ref.dtype)
    y_ref[...] = y_ref[...] + _rms(down_ref[...], g3_ref[...])


def _sample_attend(pg, n_pages, db, b, pt_ref, ck_hbm, cv_hbm, clf_hbm, q, kn32, vn32, cn8,
                   kbuf, vbuf, lfbuf, ksem, vsem, lfsem,
                   qbd_ref, kcat_ref, vcat_ref, m_ref, l_ref, acc_ref, hooks):
    ng = n_pages // pg
    t = q.shape[0]
    nrow = t * H_A
    width = pg * PAGE

    def page_id(seq, j):
        return pt_ref[(n_pages - 1 - j) * db + seq]

    def kv_copies(pid, slot, i):
        return (pltpu.make_async_copy(ck_hbm.at[pid], kbuf.at[slot, i], ksem.at[slot]),
                pltpu.make_async_copy(cv_hbm.at[pid], vbuf.at[slot, i], vsem.at[slot]))

    def lf_copy(pid, lslot, j):
        return pltpu.make_async_copy(clf_hbm.at[pid], lfbuf.at[lslot, j], lfsem.at[lslot])

    def start_kv(seq, g, slot):
        for i in range(pg):
            for cp in kv_copies(page_id(seq, g * pg + i), slot, i):
                cp.start()

    def wait_kv(slot):
        for i in range(pg):
            for cp in kv_copies(0, slot, i):
                cp.wait()

    def start_lf(seq, lslot):
        for j in range(n_pages):
            lf_copy(page_id(seq, j), lslot, j).start()

    lslot = b % 2

    @pl.when(b == 0)
    def _prime():
        start_lf(0, 0)
        start_kv(0, 0, 0)

    for j in range(n_pages):
        lf_copy(0, lslot, j).wait()

    @pl.when(b + 1 < db)
    def _next_lf():
        start_lf(b + 1, 1 - lslot)

    hsel = (lax.broadcasted_iota(jnp.int32, (H_A, MIX_A), 1) // HD_A
            == lax.broadcasted_iota(jnp.int32, (H_A, MIX_A), 0))
    qbd = jnp.where(hsel[None], q[:, None, :], 0.0)
    qbd_ref[...] = qbd.reshape(nrow, MIX_A).astype(bf16)

    run = jnp.zeros((H_A, LANES), f32)
    for g in range(ng):
        slot = g % 2
        if g + 1 < ng:
            start_kv(b, g + 1, 1 - slot)
        else:
            @pl.when(b + 1 < db)
            def _next_seq():
                start_kv(b + 1, 0, 1 - slot)
        wait_kv(slot)
        if hooks is not None:
            hooks[0](g)
        biases = []
        for i in range(pg):
            kcat_ref[:, i * PAGE:(i + 1) * PAGE] = kbuf[slot, i].reshape(MIX_A, PAGE).astype(bf16)
            vcat_ref[:, i * PAGE:(i + 1) * PAGE] = vbuf[slot, i].reshape(MIX_A, PAGE).astype(bf16)
            lf = lfbuf[lslot, g * pg + i]
            incl = _suffix_sums(lf)
            biases.append(incl - lf + run)
            run = run + incl[:, 0:1]
        bias = jnp.concatenate(biases, axis=1) * LOG2E
        s = _dot(qbd_ref[...], kcat_ref[...])
        if hooks is not None:
            hooks[1](g)
        s = (s.reshape(t, H_A, width) + bias[None]).reshape(nrow, width)
        m_cur = jnp.max(s, axis=1, keepdims=True)
        if g == 0:
            m_next = jnp.broadcast_to(m_cur, (nrow, LANES))
        else:
            m_prev = m_ref[...]
            m_next = jnp.maximum(m_prev, m_cur)
        p = jnp.exp2(s - _tile_lanes(m_next, width // LANES))
        p_sum = jnp.sum(p, axis=1, keepdims=True)
        pv = _dot_nt(p.astype(bf16), vcat_ref[...])
        if g == 0:
            l_ref[...] = jnp.broadcast_to(p_sum, (nrow, LANES))
            acc_ref[...] = pv
        else:
            a = jnp.exp2(m_prev - m_next)
            l_ref[...] = a * l_ref[...] + p_sum
            acc_ref[...] = acc_ref[...] * _tile_lanes(a, MIX_A // LANES) + pv
        m_ref[...] = m_next
    if hooks is not None:
        hooks[0](ng)

    pad = jnp.zeros((t, MIX_A), f32)
    kn = jnp.concatenate([kn32, pad], axis=0).astype(bf16)
    vn = jnp.concatenate([vn32, pad], axis=0).astype(bf16)
    cn = jnp.concatenate([cn8, jnp.zeros((H_A, t), f32)], axis=1) * LOG2E
    sn = _dot_nt(qbd_ref[...], kn)
    sn = (sn.reshape(t, H_A, 2 * t) - cn[None]).reshape(nrow, 2 * t)
    row_t = lax.broadcasted_iota(jnp.int32, (nrow, 2 * t), 0) // H_A
    col = lax.broadcasted_iota(jnp.int32, (nrow, 2 * t), 1)
    sn = jnp.where(col <= row_t, sn, NEG)
    m_prev = m_ref[...]
    m_next = jnp.maximum(m_prev, jnp.max(sn, axis=1, keepdims=True))
    a = jnp.exp2(m_prev - m_next)
    pn = jnp.exp2(sn - m_next[:, 0:1])
    l = a * l_ref[...] + jnp.sum(pn, axis=1, keepdims=True)
    acc = acc_ref[...] * _tile_lanes(a, MIX_A // LANES) + _dot(pn.astype(bf16), vn)
    o = acc / _tile_lanes(l, MIX_A // LANES)
    o3 = jnp.where(hsel[None], o.reshape(t, H_A, MIX_A), 0.0)
    return jnp.sum(o3, axis=1)


def _mlp_sample(x, oa, ob, wo, wu, wd, g1, g2, g3, pt_flat, ck, cv, clf, q, kn, vn, cnt, db, t, pg, tm):
    rows = x.shape[0]
    nsteps = rows // tm
    spp = db // nsteps
    n_pages = pt_flat.shape[0] // db
    ng = n_pages // pg
    assert db % nsteps == 0 and ng % 2 == 0 and D_FF % ng == 0
    nrow = t * H_A
    row_spec = lambda width: pl.BlockSpec((tm, width), lambda i, pt: (i, 0))
    const = lambda shape: pl.BlockSpec(shape, lambda i, pt: (0, 0), pipeline_mode=pl.Buffered(1))
    seq_spec = lambda width: pl.BlockSpec((1, spp * t, width), lambda i, pt: (0, i, 0))
    hbm = pl.BlockSpec(memory_space=pl.ANY)
    grid_spec = pltpu.PrefetchScalarGridSpec(
        num_scalar_prefetch=1,
        grid=(nsteps,),
        in_specs=[row_spec(D_MODEL), row_spec(MIX_A), row_spec(MIX_B),
                  const((D_MODEL, D_MODEL)), const((D_MODEL, D_FF)), const((D_FF, D_MODEL)),
                  const((1, D_MODEL)), const((1, D_MODEL)), const((1, D_MODEL)),
                  hbm, hbm, hbm, seq_spec(MIX_A), seq_spec(MIX_A), seq_spec(MIX_A),
                  pl.BlockSpec((spp, H_A, t), lambda i, pt: (i, 0, 0))],
        out_specs=(row_spec(D_MODEL), pl.BlockSpec((spp, t, MIX_A), lambda i, pt: (i, 0, 0))),
        scratch_shapes=[
            pltpu.VMEM((2, pg, H_A, HD_A, PAGE), f32),
            pltpu.VMEM((2, pg, H_A, HD_A, PAGE), f32),
            pltpu.VMEM((2, n_pages, H_A, PAGE), f32),
            pltpu.SemaphoreType.DMA((2,)),
            pltpu.SemaphoreType.DMA((2,)),
            pltpu.SemaphoreType.DMA((2,)),
            pltpu.VMEM((nrow, MIX_A), bf16),
            pltpu.VMEM((MIX_A, pg * PAGE), bf16),
            pltpu.VMEM((MIX_A, pg * PAGE), bf16),
            pltpu.VMEM((nrow, LANES), f32),
            pltpu.VMEM((nrow, LANES), f32),
            pltpu.VMEM((nrow, MIX_A), f32),
            pltpu.VMEM((tm, D_MODEL), bf16),
            pltpu.VMEM((tm, D_FF // ng), bf16),
            pltpu.VMEM((tm, D_MODEL), f32),
        ],
    )
    return pl.pallas_call(
        functools.partial(_mlp_sample_kernel, pg, n_pages, db, spp),
        grid_spec=grid_spec,
        out_shape=(jax.ShapeDtypeStruct((rows, D_MODEL), f32),
                   jax.ShapeDtypeStruct((db, t, MIX_A), bf16)),
        compiler_params=_cparams(("arbitrary",)),
        name="mlp_sample",
    )(pt_flat, x, oa, ob, wo, wu, wd, g1, g2, g3, ck, cv, clf, q, kn, vn, cnt)


def _largest_divisor(n, candidates):
    for c in candidates:
        if n % c == 0:
            return c
    raise ValueError(f"no tile size for {n}")


def _rope_tables(pos):
    half = DK_B // 2
    inv = ROPE_BASE ** (-jnp.arange(half, dtype=f32) / half)
    ang = pos[:, None] * inv[None, :]
    cos = jnp.cos(ang)
    sin = jnp.sin(ang)
    return (jnp.concatenate([cos, cos, cos, cos], axis=1),
            jnp.concatenate([-sin, sin, -sin, sin], axis=1))


def _sum_matrix(seg_id, suffix):
    i = np.arange(seg_id.shape[0])
    same = seg_id[:, None] == seg_id[None, :]
    prefix = same & (i[None, :] <= i[:, None]) & ~suffix[:, None]
    later = same & (i[None, :] > i[:, None]) & suffix[:, None]
    return jnp.asarray(prefix.astype(np.float32) - later.astype(np.float32), bf16)


def kernel(x_prompt, x_sample, cache_k, cache_v, cache_logf, state_ret, page_table, meta_tokens,
           g_pre_mix, w_in, b_f, w_out, g_post_mix, g_pre_mlp, w_up, w_down, g_post_mlp):
    nb, seq, _ = x_prompt.shape
    db, t, _ = x_sample.shape
    n_pool = cache_k.shape[1]
    n_pages = page_table.shape[1]
    n_s = db * t
    assert w_in.shape[0] == 1, "single layer"
    assert seq % 256 == 0 and n_s % LANES == 0 and cache_k.shape[2] == PAGE and n_pages % 16 == 0

    w = w_in[0]
    n_fa = 3 * MIX_A
    w_p = jnp.concatenate([w[:, :n_fa], w[:, n_fa + H_A:], w[:, n_fa:n_fa + H_A],
                           jnp.zeros((D_MODEL, D_INP - w.shape[1]), w.dtype)], axis=1).astype(bf16)
    bf_p = jnp.concatenate([b_f[0], jnp.zeros((LANES - H_A,), f32)])[None]
    g0 = g_pre_mix[0][None]
    wo = w_out[0].astype(bf16)
    wu = w_up[0].astype(bf16)
    wd = w_down[0].astype(bf16)
    g1, g2, g3 = g_post_mix[0][None], g_pre_mlp[0][None], g_post_mlp[0][None]

    rows_aux = n_s + LANES
    n_pad = LANES - N_META
    x_aux = jnp.concatenate([x_sample.reshape(n_s, D_MODEL), meta_tokens,
                             jnp.zeros((n_pad, D_MODEL), f32)], axis=0)[None]
    past = n_pages * PAGE
    pos_aux = jnp.concatenate([jnp.tile(past + jnp.arange(t, dtype=f32), db),
                               jnp.arange(N_META, dtype=f32), jnp.zeros((n_pad,), f32)])
    cos_a, sin_a = _rope_tables(pos_aux)
    seg_aux = np.concatenate([np.arange(n_s) // t, np.full((N_META,), db), np.full((n_pad,), db + 1)])
    is_meta = seg_aux == db
    (q_a, kaug_a, vt_a, k32_a, v32_a, lf_a, ccol_a, qr_a, kr_a, vr_a, gr_a) = _proj(
        False, x_aux, g0, w_p, bf_p, cos_a, sin_a, _sum_matrix(seg_aux, is_meta), rows_aux)
    meta_blk16 = n_s // N_META
    meta_blk128 = n_s // LANES

    tm = _largest_divisor(seq, (512, 256))
    cos_p, sin_p = _rope_tables(N_META + jnp.arange(seq, dtype=f32))
    one_seg = np.zeros((tm,), np.int64)
    (qaug_p, kaug_p, vt_p, k32_p, v32_p, lf_p, qr_p, kr_p, vr_p, gr_p) = _proj(
        True, x_prompt, g0, w_p, bf_p, cos_p, sin_p, _sum_matrix(one_seg, one_seg > 0), tm,
        meta=(k32_a, v32_a, lf_a, meta_blk16))

    oa_p = _fox_prompt(qaug_p, kaug_p, vt_p, kaug_a, vt_a, meta_blk128, tm)
    ob_p, s_p = _ret_prompt(qr_p, kr_p, vr_p, gr_p, kr_a, vr_a, meta_blk16, 256)

    cnt = ccol_a[0, :n_s].reshape(db, t, H_A).transpose(0, 2, 1)
    pg = 16 if n_pages % 32 == 0 else 8
    tm_mlp = 256
    y_p, oa_s = _mlp_sample(x_prompt.reshape(nb * seq, D_MODEL), oa_p.reshape(nb * seq, MIX_A),
                            ob_p.reshape(nb * seq, MIX_B), wo, wu, wd, g1, g2, g3,
                            page_table.T.reshape(n_pages * db),
                            cache_k[0].transpose(0, 2, 3, 1), cache_v[0].transpose(0, 2, 3, 1),
                            cache_logf[0].transpose(0, 2, 1),
                            q_a, k32_a, v32_a, cnt, db, t, pg, tm_mlp)

    sb = _largest_divisor(db, (8, 4, 2))
    ob_s, s_s = _ret_sample(qr_a, kr_a, vr_a, gr_a, state_ret[0], db, t, sb)
    tms = _largest_divisor(n_s, (512, 256, 128))
    y_s = _mlp(x_sample.reshape(n_s, D_MODEL), oa_s.reshape(n_s, MIX_A), ob_s[0],
               wo, wu, wd, g1, g2, g3, tms)

    k_prompt = k32_p.reshape(1, nb, N_META + seq, H_A, HD_A)
    v_prompt = v32_p.reshape(1, nb, N_META + seq, H_A, HD_A)
    return (y_p.reshape(nb, seq, D_MODEL), y_s.reshape(db, t, D_MODEL),
            k_prompt, v_prompt, lf_p[None], s_p[None],
            k32_a[0, :n_s].reshape(1, db, t, H_A, HD_A), v32_a[0, :n_s].reshape(1, db, t, H_A, HD_A),
            lf_a[0, :n_s].reshape(1, db, t, H_A), s_s[None])
```

```python
import functools

import numpy as np
import jax
import jax.numpy as jnp
from jax import lax
from jax.experimental import pallas as pl
from jax.experimental.pallas import tpu as pltpu

f32 = jnp.float32
bf16 = jnp.bfloat16

D_MODEL = 1024
N_META = 16
PAGE = 128
HD_A = 64
H_A = 8
H_B = 4
DK_B = 64
DV_B = 128
MIX_A = H_A * HD_A
MIX_B = H_B * DV_B
D_FF = 4 * D_MODEL
ROPE_BASE = 10000.0
EPS = 1e-6
NEG = -1e30
LOG2E = 1.4426950408889634

QA, KA, VA, QR, KR, VR, GR, FA = 0, 512, 1024, 1536, 1792, 2048, 2560, 3072
D_INP = 3200
LANES = 128
SUBLANES = 8
AUG = H_A * LANES
ONE_LANE = 3 * H_A
VT_ROWS = 80
SCORES_AHEAD = 3
KV_TILE = 1024
Q_TILE = 512
KV_GROUP = 8
KV_SLOTS = 4

VMEM_LIMIT = 56 * 1024 * 1024


def _cparams(sem):
    return pltpu.CompilerParams(dimension_semantics=sem, vmem_limit_bytes=VMEM_LIMIT)


def _dot(a, b):
    return jnp.dot(a, b, preferred_element_type=f32)


def _dot_nt(a, b):
    return lax.dot_general(a, b, (((1,), (1,)), ((), ())), preferred_element_type=f32)


def _dot_tn(a, b):
    return lax.dot_general(a, b, (((0,), (0,)), ((), ())), preferred_element_type=f32)


def _split3(x):
    hi = x.astype(bf16).astype(f32)
    r = x - hi
    mid = r.astype(bf16).astype(f32)
    lo = (r - mid).astype(bf16).astype(f32)
    return hi, mid, lo


def _pack3(x, lane):
    hi, mid, lo = _split3(x)
    return (hi + pltpu.roll(mid, H_A, 1) + pltpu.roll(lo, 2 * H_A, 1)
            + jnp.where(lane == ONE_LANE, 1.0, 0.0))


def _rms(x, g):
    return x * lax.rsqrt(jnp.mean(x * x, axis=-1, keepdims=True) + EPS) * g


def _own_lanes(g, lane):
    return (lane < HD_A) if g % 2 == 0 else (lane >= HD_A)


def _placements():
    rk = np.zeros((LANES, AUG), np.float32)
    rq = np.zeros((LANES, AUG), np.float32)
    for g in range(H_A):
        base = g * LANES + (HD_A if g % 2 == 0 else 0)
        for part in range(3):
            rk[part * H_A + g, base + part] = 1.0
            rk[ONE_LANE, base + 3 + part] = 1.0
            rq[ONE_LANE, base + part] = 1.0
            rq[part * H_A + g, base + 3 + part] = 1.0
    return jnp.asarray(rk, bf16), jnp.asarray(rq, bf16)


def _proj_kernel(prompt, x_ref, g_ref, w_ref, bf_ref, cos_ref, sin_ref, a_ref, rk_ref, rq_ref, *refs):
    if prompt:
        (km_ref, vm_ref, lfm_ref, qaug_ref, kaug_ref, vt_ref, k_hbm, v_hbm, lf_hbm,
         qr_ref, kr_ref, vr_ref, gr_ref, carry_ref, kst_ref, vst_ref, lst_ref, osem, msem) = refs
    else:
        (q_ref, kaug_ref, vt_ref, k32_ref, v32_ref, lf_ref, ccol_ref,
         qr_ref, kr_ref, vr_ref, gr_ref, carry_ref) = refs
    t = pl.program_id(1)
    x = x_ref[0]
    h = _rms(x, g_ref[...]).astype(bf16)
    z = _dot(h, w_ref[...])
    tm = z.shape[0]
    lane = lax.broadcasted_iota(jnp.int32, (tm, LANES), 1)

    qs = z[:, QA:QA + MIX_A] * (HD_A ** -0.5 * LOG2E)
    k = z[:, KA:KA + MIX_A]
    v = z[:, VA:VA + MIX_A]
    if not prompt:
        k32_ref[0] = k
        v32_ref[0] = v
    vt3 = v.T.reshape(H_A, HD_A, tm)
    extra = jnp.where(lax.broadcasted_iota(jnp.int32, (H_A, VT_ROWS - HD_A, tm), 1) == 0, 1.0, 0.0)
    vt_ref[0] = jnp.concatenate([vt3, extra], axis=1).reshape(H_A * VT_ROWS, tm).astype(bf16)
    vr_ref[0] = z[:, VR:VR + MIX_B].astype(bf16)
    gr_ref[0] = z[:, GR:GR + MIX_B]

    cos = cos_ref[...]
    sin = sin_ref[...]
    first_half = (lane % DK_B) < (DK_B // 2)
    for s in range(4):
        zs = z[:, QR + s * LANES: QR + (s + 1) * LANES]
        partner = jnp.where(first_half, pltpu.roll(zs, LANES - DK_B // 2, 1),
                            pltpu.roll(zs, DK_B // 2, 1))
        r = zs * cos + partner * sin
        if s < 2:
            qr_ref[0, :, s * LANES:(s + 1) * LANES] = r
        else:
            kr_ref[0, :, (s - 2) * LANES:(s - 1) * LANES] = r * (DK_B ** -0.5)

    fa = z[:, FA:FA + LANES] + bf_ref[...]
    lf = jnp.where(lane < H_A, jnp.minimum(fa, 0.0) - jnp.log1p(jnp.exp(-jnp.abs(fa))), 0.0)
    if prompt:
        nt = pl.num_programs(1)
        step = pl.program_id(0) * nt + t
        slot = step % 2

        def out_copies(slot_, b_, t_):
            rows = pl.ds(N_META + t_ * tm, tm)
            return (pltpu.make_async_copy(kst_ref.at[slot_], k_hbm.at[b_, rows], osem.at[0, slot_]),
                    pltpu.make_async_copy(vst_ref.at[slot_], v_hbm.at[b_, rows], osem.at[1, slot_]),
                    pltpu.make_async_copy(lst_ref.at[slot_], lf_hbm.at[b_, rows], osem.at[2, slot_]))

        @pl.when(step >= 2)
        def _reuse():
            for cp in out_copies(slot, 0, 0):
                cp.wait()

        kst_ref[slot] = k
        vst_ref[slot] = v
        lst_ref[slot] = lf[:, :H_A]
        for cp in out_copies(slot, pl.program_id(0), t):
            cp.start()

        @pl.when(t == 0)
        def _meta_rows():
            head = pl.ds(0, N_META)
            cps = (pltpu.make_async_copy(km_ref.at[0], k_hbm.at[pl.program_id(0), head], msem.at[0]),
                   pltpu.make_async_copy(vm_ref.at[0], v_hbm.at[pl.program_id(0), head], msem.at[1]),
                   pltpu.make_async_copy(lfm_ref.at[0], lf_hbm.at[pl.program_id(0), head], msem.at[2]))
            for cp in cps:
                cp.start()
            for cp in cps:
                cp.wait()

        @pl.when(step == pl.num_programs(0) * nt - 1)
        def _drain():
            for cp in out_copies(slot, 0, 0) + out_copies(1 - slot, 0, 0):
                cp.wait()
    else:
        lf_ref[0] = lf[:, :H_A]
    hi, mid, lo = _split3(lf)
    parts = (hi + pltpu.roll(mid, H_A, 1) + pltpu.roll(lo, 2 * H_A, 1)).astype(bf16)
    cs = _dot(a_ref[...], parts)
    c = cs + pltpu.roll(cs, LANES - H_A, 1) + pltpu.roll(cs, LANES - 2 * H_A, 1)

    @pl.when(t == 0)
    def _():
        carry_ref[...] = jnp.zeros_like(carry_ref)

    carry = carry_ref[...]
    c = jnp.where(lane < H_A, c + carry[0:1], 0.0)
    carry_ref[...] = jnp.broadcast_to(c[tm - 1:tm], carry_ref.shape)

    e_k = _dot(_pack3(-LOG2E * c, lane).astype(bf16), rk_ref[...])
    for g in range(H_A):
        pair = k[:, (g // 2) * LANES:(g // 2 + 1) * LANES]
        kaug_ref[0, :, g * LANES:(g + 1) * LANES] = jnp.where(
            _own_lanes(g, lane), pair, e_k[:, g * LANES:(g + 1) * LANES]).astype(bf16)
    if prompt:
        lane8 = lane[:SUBLANES]
        qx = _pack3(LOG2E * carry, lane8)
        e_q = _dot(jnp.concatenate([qx, qx], axis=0).astype(bf16), rq_ref[...])
        for g in range(H_A):
            pair = qs[:, (g // 2) * LANES:(g // 2 + 1) * LANES]
            qaug_ref[0, :, g * LANES:(g + 1) * LANES] = jnp.where(
                _own_lanes(g, lane), pair, e_q[0:1, g * LANES:(g + 1) * LANES]).astype(bf16)
    else:
        q_ref[0] = qs
        ccol_ref[0] = c[:, :H_A]


def _proj(prompt, x, g, w, bfp, cos_t, sin_t, a, tm, meta=None):
    nb, rows, _ = x.shape
    nt = rows // tm
    rk, rq = _placements()
    row_spec = lambda width: pl.BlockSpec((1, tm, width), lambda b, t: (b, t, 0))
    const = lambda shape: pl.BlockSpec(shape, lambda b, t: (0,) * len(shape))
    rows_of = lambda width, dt: jax.ShapeDtypeStruct((nb, rows, width), dt)
    vt_shape = jax.ShapeDtypeStruct((nb, H_A * VT_ROWS, rows), bf16)
    vt_spec = pl.BlockSpec((1, H_A * VT_ROWS, tm), lambda b, t: (b, 0, t))
    tail_shape = (rows_of(H_B * DK_B, f32), rows_of(H_B * DK_B, f32), rows_of(MIX_B, bf16), rows_of(MIX_B, f32))
    tail_spec = (row_spec(H_B * DK_B), row_spec(H_B * DK_B), row_spec(MIX_B), row_spec(MIX_B))
    extra_in, extra_specs = (), []
    scratch = [pltpu.VMEM((SUBLANES, LANES), f32)]
    if prompt:
        assert nb * nt >= 2
        k_a, v_a, lf_a, meta_blk = meta
        full = lambda width: jax.ShapeDtypeStruct((nb, N_META + rows, width), f32)
        hbm = pl.BlockSpec(memory_space=pl.ANY)
        out_shape = (rows_of(AUG, bf16), rows_of(AUG, bf16), vt_shape, full(MIX_A), full(MIX_A),
                     full(H_A)) + tail_shape
        out_specs = (row_spec(AUG), row_spec(AUG), vt_spec, hbm, hbm, hbm) + tail_spec
        meta_spec = lambda width: pl.BlockSpec((1, N_META, width), lambda b, t: (0, meta_blk, 0))
        extra_in = (k_a, v_a, lf_a)
        extra_specs = [meta_spec(MIX_A), meta_spec(MIX_A), meta_spec(H_A)]
        scratch += [pltpu.VMEM((2, tm, MIX_A), f32), pltpu.VMEM((2, tm, MIX_A), f32),
                    pltpu.VMEM((2, tm, H_A), f32),
                    pltpu.SemaphoreType.DMA((3, 2)), pltpu.SemaphoreType.DMA((3,))]
    else:
        out_shape = (rows_of(MIX_A, f32), rows_of(AUG, bf16), vt_shape, rows_of(MIX_A, f32), rows_of(MIX_A, f32),
                     rows_of(H_A, f32), rows_of(H_A, f32)) + tail_shape
        out_specs = (row_spec(MIX_A), row_spec(AUG), vt_spec, row_spec(MIX_A), row_spec(MIX_A),
                     row_spec(H_A), row_spec(H_A)) + tail_spec
    return pl.pallas_call(
        functools.partial(_proj_kernel, prompt),
        grid=(nb, nt),
        in_specs=[
            row_spec(D_MODEL),
            const((1, D_MODEL)),
            const((D_MODEL, D_INP)),
            const((1, LANES)),
            pl.BlockSpec((tm, LANES), lambda b, t: (t, 0)),
            pl.BlockSpec((tm, LANES), lambda b, t: (t, 0)),
            const((tm, tm)),
            const((LANES, AUG)),
            const((LANES, AUG)),
        ] + extra_specs,
        out_specs=out_specs,
        out_shape=out_shape,
        scratch_shapes=scratch,
        compiler_params=_cparams(("arbitrary", "arbitrary")),
        name="proj_prompt" if prompt else "proj_aux",
    )(x, g, w, bfp, cos_t, sin_t, a, rk, rq, *extra_in)


def _fox_prompt_kernel(r, qi_tab, kj_tab, mode_tab, q_ref, k_ref, vt_ref, km_ref, vtm_ref, o_ref,
                       m_ref, acc_ref):
    p_id = pl.program_id(1)
    kj = kj_tab[p_id]
    mode = mode_tab[p_id]
    tq = q_ref.shape[1]

    def attend(kk_ref, vv_ref, keys, keep, first):
        ks = keys.stop - keys.start
        m_new, scale, pvs = [], [], []

        def scores(h):
            cols = slice(h * LANES, (h + 1) * LANES)
            return _dot_nt(kk_ref[0, keys, cols], q_ref[0, :, cols])

        ahead = [scores(h) for h in range(SCORES_AHEAD)]
        for h in range(H_A):
            st = ahead.pop(0)
            if h + SCORES_AHEAD < H_A:
                ahead.append(scores(h + SCORES_AHEAD))
            if keep is not None:
                st = jnp.where(keep, st, NEG)
            s3 = st.reshape(ks // SUBLANES, SUBLANES, tq)
            m_cur = jnp.max(jnp.max(s3, axis=0), axis=0, keepdims=True)
            if first:
                m_next = jnp.broadcast_to(m_cur, (SUBLANES, tq))
            else:
                m_prev = m_ref[h]
                m_next = jnp.maximum(m_prev, m_cur)
                a = jnp.exp2(m_prev - m_next)
                scale.append(jnp.broadcast_to(a[None], (VT_ROWS // SUBLANES, SUBLANES, tq)).reshape(VT_ROWS, tq))
            p = jnp.exp2(s3 - m_next[None]).reshape(ks, tq).astype(bf16)
            pvs.append(_dot(vv_ref[0, h * VT_ROWS:(h + 1) * VT_ROWS, keys], p))
            m_new.append(m_next)
        m_ref[...] = jnp.stack(m_new)
        pv = jnp.concatenate(pvs, axis=0)
        if first:
            acc_ref[...] = pv
        else:
            acc_ref[...] = acc_ref[...] * jnp.concatenate(scale, axis=0) + pv

    block = lambda s: slice(s * tq, (s + 1) * tq)

    @pl.when(kj == 0)
    def _meta():
        nk = km_ref.shape[1]
        attend(km_ref, vtm_ref, slice(0, nk), lax.broadcasted_iota(jnp.int32, (nk, tq), 0) < N_META, True)

    for s in range(r):
        @pl.when((mode == 0) | (mode > s + 1))
        def _before():
            attend(k_ref, vt_ref, block(s), None, False)

    causal = (lax.broadcasted_iota(jnp.int32, (tq, tq), 0) <= lax.broadcasted_iota(jnp.int32, (tq, tq), 1))
    for d in range(1, r + 1):
        @pl.when(mode == d)
        def _last():
            attend(k_ref, vt_ref, block(d - 1), causal, False)
            acc3 = acc_ref[...].reshape(H_A, VT_ROWS, tq)
            o = acc3[:, :HD_A] / acc3[:, HD_A:HD_A + 1]
            o_ref[0] = o.reshape(MIX_A, tq).T.astype(o_ref.dtype)


def _fox_prompt(q_aug, k_aug, vt, km_src, vtm_src, meta_blk, tq):
    nb, rows, _ = q_aug.shape
    nq = rows // tq
    r = max(d for d in (1, 2, 4) if nq % d == 0 and d * tq <= KV_TILE)
    tk = r * tq
    qi_np, kj_np, mode_np = [], [], []
    for i in range(nq):
        for j in range(i // r + 1):
            qi_np.append(i)
            kj_np.append(j)
            mode_np.append(i % r + 1 if j == i // r else 0)
    tabs = [jnp.asarray(np.array(a, np.int32)) for a in (qi_np, kj_np, mode_np)]
    vrows = H_A * VT_ROWS
    grid_spec = pltpu.PrefetchScalarGridSpec(
        num_scalar_prefetch=3,
        grid=(nb, len(qi_np)),
        in_specs=[
            pl.BlockSpec((1, tq, AUG), lambda b, p, qt, kt, mt: (b, qt[p], 0)),
            pl.BlockSpec((1, tk, AUG), lambda b, p, qt, kt, mt: (b, kt[p], 0)),
            pl.BlockSpec((1, vrows, tk), lambda b, p, qt, kt, mt: (b, 0, kt[p])),
            pl.BlockSpec((1, LANES, AUG), lambda b, p, qt, kt, mt: (0, meta_blk, 0)),
            pl.BlockSpec((1, vrows, LANES), lambda b, p, qt, kt, mt: (0, 0, meta_blk)),
        ],
        out_specs=pl.BlockSpec((1, tq, MIX_A), lambda b, p, qt, kt, mt: (b, qt[p], 0)),
        scratch_shapes=[
            pltpu.VMEM((H_A, SUBLANES, tq), f32),
            pltpu.VMEM((vrows, tq), f32),
        ],
    )
    return pl.pallas_call(
        functools.partial(_fox_prompt_kernel, r),
        grid_spec=grid_spec,
        out_shape=jax.ShapeDtypeStruct((nb, rows, MIX_A), bf16),
        compiler_params=_cparams(("arbitrary", "arbitrary")),
        name="fox_prompt",
    )(*tabs, q_aug, k_aug, vt, km_src, vtm_src)


def _log_gamma():
    return np.log1p(-np.exp2(-5.0 - np.arange(H_B, dtype=np.float64)))


def _ret_tables(c):
    lg = _log_gamma()
    n = np.arange(c, dtype=np.float64)
    rel = n[:, None] - n[None, :]
    dec = np.where(rel >= 0, np.exp(np.maximum(rel, 0.0)[None] * lg[:, None, None]), 0.0)
    q_dec = np.exp((n + 1.0)[None, :] * lg[:, None])
    k_dec = np.exp((c - 1.0 - n)[None, :] * lg[:, None])
    q_dec_full = np.broadcast_to(q_dec[:, :, None], (H_B, c, DV_B))
    k_dec_full = np.repeat(k_dec.T, DK_B, axis=1)
    g_c = [float(v) for v in np.exp(c * lg)]
    return (jnp.asarray(dec, f32), jnp.asarray(q_dec_full, f32), jnp.asarray(k_dec_full, f32), g_c)


def _head_norm_gate(o, gate):
    mu = jnp.mean(o, axis=-1, keepdims=True)
    d = o - mu
    var = jnp.mean(d * d, axis=-1, keepdims=True)
    y = d * lax.rsqrt(var + EPS)
    return y * (gate * jax.nn.sigmoid(gate))


def _ret_prompt_kernel(g_c, qr_ref, kr_ref, vr_ref, gr_ref, dec_ref, qdec_ref, kdec_ref,
                       krm_ref, vrm_ref, kdecm_ref, ob_ref, s_out_ref, s_ref):
    c = pl.program_id(1)

    @pl.when(c == 0)
    def _init():
        kd = (krm_ref[0] * kdecm_ref[...]).astype(bf16)
        for h in range(H_B):
            s_ref[h] = _dot_tn(kd[:, h * DK_B:(h + 1) * DK_B], vrm_ref[0, :, h * DV_B:(h + 1) * DV_B])

    q = qr_ref[0].astype(bf16)
    k = kr_ref[0]
    kb = k.astype(bf16)
    kd = (k * kdec_ref[...]).astype(bf16)
    heads = range(H_B)
    dk = lambda h: slice(h * DK_B, (h + 1) * DK_B)
    dv = lambda h: slice(h * DV_B, (h + 1) * DV_B)
    s_old = [s_ref[h] for h in heads]
    inner = [_dot_nt(q[:, dk(h)], kb[:, dk(h)]) for h in heads]
    cross = [_dot(q[:, dk(h)], s_old[h].astype(bf16)) for h in heads]
    s_ref[...] = jnp.stack([g_c[h] * s_old[h] + _dot_tn(kd[:, dk(h)], vr_ref[0, :, dv(h)]) for h in heads])
    for h in heads:
        o = _dot((inner[h] * dec_ref[h]).astype(bf16), vr_ref[0, :, dv(h)]) + cross[h] * qdec_ref[h]
        ob_ref[0, :, dv(h)] = _head_norm_gate(o, gr_ref[0, :, dv(h)]).astype(ob_ref.dtype)

    @pl.when(c == pl.num_programs(1) - 1)
    def _fin():
        s_out_ref[0] = s_ref[...]


def _ret_prompt(qr, kr, vr, gr, krm_src, vrm_src, meta_blk, chunk):
    nb, rows, _ = qr.shape
    nc = rows // chunk
    dec, qdec, kdec, g_c = _ret_tables(chunk)
    _, _, kdec_m, _ = _ret_tables(N_META)
    row_spec = lambda width: pl.BlockSpec((1, chunk, width), lambda b, c: (b, c, 0))
    const = lambda shape: pl.BlockSpec(shape, lambda b, c: (0,) * len(shape))
    return pl.pallas_call(
        functools.partial(_ret_prompt_kernel, g_c),
        grid=(nb, nc),
        in_specs=[
            row_spec(H_B * DK_B), row_spec(H_B * DK_B), row_spec(MIX_B), row_spec(MIX_B),
            const((H_B, chunk, chunk)), const((H_B, chunk, DV_B)), const((chunk, H_B * DK_B)),
            pl.BlockSpec((1, N_META, H_B * DK_B), lambda b, c: (0, meta_blk, 0)),
            pl.BlockSpec((1, N_META, MIX_B), lambda b, c: (0, meta_blk, 0)),
            const((N_META, H_B * DK_B)),
        ],
        out_specs=(row_spec(MIX_B),
                   pl.BlockSpec((1, H_B, DK_B, DV_B), lambda b, c: (b, 0, 0, 0))),
        out_shape=(jax.ShapeDtypeStruct((nb, rows, MIX_B), bf16),
                   jax.ShapeDtypeStruct((nb, H_B, DK_B, DV_B), f32)),
        scratch_shapes=[pltpu.VMEM((H_B, DK_B, DV_B), f32)],
        compiler_params=_cparams(("arbitrary", "arbitrary")),
        name="ret_prompt",
    )(qr, kr, vr, gr, dec, qdec, kdec, krm_src, vrm_src, kdec_m)


def _ret_sample_kernel(g_c, sb, qr_ref, kr_ref, vr_ref, gr_ref, st_ref, dec_ref, qdec_ref, kdec_ref,
                       ob_ref, s_out_ref):
    t = qr_ref.shape[1] // sb
    q = qr_ref[0].reshape(sb, t, H_B * DK_B)
    k = kr_ref[0].reshape(sb, t, H_B * DK_B)
    kd = k * kdec_ref[...][None]
    v = vr_ref[0].astype(f32).reshape(sb, t, MIX_B)
    gate = gr_ref[0].reshape(sb, t, MIX_B)
    for h in range(H_B):
        qh = q[:, :, h * DK_B:(h + 1) * DK_B]
        kh = k[:, :, h * DK_B:(h + 1) * DK_B]
        kdh = kd[:, :, h * DK_B:(h + 1) * DK_B]
        vh = v[:, :, h * DV_B:(h + 1) * DV_B]
        s_old = st_ref[:, h]
        inner = jnp.einsum('btd,bsd->bts', qh, kh, preferred_element_type=f32) * dec_ref[h][None]
        o = (jnp.einsum('bts,bse->bte', inner, vh, preferred_element_type=f32)
             + jnp.einsum('btd,bde->bte', qh, s_old, preferred_element_type=f32)
             * qdec_ref[h][None])
        s_out_ref[:, h] = g_c[h] * s_old + jnp.einsum('btd,bte->bde', kdh, vh,
                                                      preferred_element_type=f32)
        y = _head_norm_gate(o, gate[:, :, h * DV_B:(h + 1) * DV_B])
        ob_ref[0, :, h * DV_B:(h + 1) * DV_B] = y.reshape(sb * t, DV_B).astype(ob_ref.dtype)


def _ret_sample(qr, kr, vr, gr, state, db, t, sb):
    dec, qdec, kdec, g_c = _ret_tables(t)
    nsteps = db // sb
    row_spec = lambda width: pl.BlockSpec((1, sb * t, width), lambda i: (0, i, 0))
    const = lambda shape: pl.BlockSpec(shape, lambda i: (0,) * len(shape))
    st_spec = pl.BlockSpec((sb, H_B, DK_B, DV_B), lambda i: (i, 0, 0, 0))
    return pl.pallas_call(
        functools.partial(_ret_sample_kernel, g_c, sb),
        grid=(nsteps,),
        in_specs=[row_spec(H_B * DK_B), row_spec(H_B * DK_B), row_spec(MIX_B), row_spec(MIX_B),
                  st_spec, const((H_B, t, t)), const((H_B, t, DV_B)), const((t, H_B * DK_B))],
        out_specs=(pl.BlockSpec((1, sb * t, MIX_B), lambda i: (0, i, 0)), st_spec),
        out_shape=(jax.ShapeDtypeStruct((1, db * t, MIX_B), bf16),
                   jax.ShapeDtypeStruct((db, H_B, DK_B, DV_B), f32)),
        compiler_params=_cparams(("arbitrary",)),
        name="ret_sample",
    )(qr, kr, vr, gr, state, dec, qdec, kdec)


def _mlp_kernel(x_ref, oa_ref, ob_ref, wo_ref, wu_ref, wd_ref, g1_ref, g2_ref, g3_ref, y_ref):
    mixed = _dot(oa_ref[...], wo_ref[0:MIX_A, :]) + _dot(ob_ref[...], wo_ref[MIX_A:, :])
    x1 = x_ref[...] + _rms(mixed, g1_ref[...])
    hn = _rms(x1, g2_ref[...]).astype(bf16)
    u = jnp.square(jnp.maximum(_dot(hn, wu_ref[...]), 0.0)).astype(bf16)
    y_ref[...] = x1 + _rms(_dot(u, wd_ref[...]), g3_ref[...])


def _mlp(x, oa, ob, wo, wu, wd, g1, g2, g3, tm):
    rows = x.shape[0]
    row_spec = lambda width: pl.BlockSpec((tm, width), lambda i: (i, 0))
    const = lambda shape: pl.BlockSpec(shape, lambda i: (0, 0), pipeline_mode=pl.Buffered(1))
    return pl.pallas_call(
        _mlp_kernel,
        grid=(rows // tm,),
        in_specs=[row_spec(D_MODEL), row_spec(MIX_A), row_spec(MIX_B),
                  const((D_MODEL, D_MODEL)), const((D_MODEL, D_FF)), const((D_FF, D_MODEL)),
                  const((1, D_MODEL)), const((1, D_MODEL)), const((1, D_MODEL))],
        out_specs=row_spec(D_MODEL),
        out_shape=jax.ShapeDtypeStruct((rows, D_MODEL), f32),
        compiler_params=_cparams(("arbitrary",)),
        name="merge_mlp",
    )(x, oa, ob, wo, wu, wd, g1, g2, g3)


def _tile_lanes(x, n):
    return jnp.concatenate([x] * n, axis=1)


def _suffix_sums(x):
    lane = lax.broadcasted_iota(jnp.int32, x.shape, 1)
    s = x
    k = 1
    while k < LANES:
        s = s + jnp.where(lane + k < LANES, pltpu.roll(s, LANES - k, 1), 0.0)
        k *= 2
    return s


def _mlp_sample_kernel(pg, n_pages, db, spp, pt_ref,
                       x_ref, oa_ref, ob_ref, wo_ref, wu_ref, wd_ref, g1_ref, g2_ref, g3_ref,
                       ck_hbm, cv_hbm, clf_hbm, q_ref, kn_ref, vn_ref, cn_ref,
                       y_ref, o_ref, kbuf, vbuf, lfbuf, ksem, vsem, lfsem,
                       qbd_ref, kcat_ref, vcat_ref, m_ref, l_ref, acc_ref, hn_ref, u_ref, down_ref):
    ng = n_pages // pg
    t = q_ref.shape[1] // spp
    ff = D_FF // ng

    def mlp_before_scores(g):
        if g == 0:
            mixed = _dot(oa_ref[...], wo_ref[0:MIX_A, :]) + _dot(ob_ref[...], wo_ref[MIX_A:, :])
            x1 = x_ref[...] + _rms(mixed, g1_ref[...])
            y_ref[...] = x1
            hn_ref[...] = _rms(x1, g2_ref[...]).astype(bf16)
        else:
            d = _dot(u_ref[...], wd_ref[(g - 1) * ff:g * ff, :])
            if g == 1:
                down_ref[...] = d
            else:
                down_ref[...] = down_ref[...] + d

    def mlp_before_softmax(g):
        u = _dot(hn_ref[...], wu_ref[:, g * ff:(g + 1) * ff])
        u_ref[...] = jnp.square(jnp.maximum(u, 0.0)).astype(bf16)

    for s in range(spp):
        seq = pl.program_id(0) * spp + s
        o_ref[s] = _sample_attend(pg, n_pages, db, seq, pt_ref, ck_hbm, cv_hbm, clf_hbm,
                                  q_ref[0, s * t:(s + 1) * t], kn_ref[0, s * t:(s + 1) * t],
                                  vn_ref[0, s * t:(s + 1) * t], cn_ref[s],
                                  kbuf, vbuf, lfbuf, ksem, vsem, lfsem,
                                  qbd_ref, kcat_ref, vcat_ref, m_ref, l_ref, acc_ref,
                                  (mlp_before_scores, mlp_before_softmax) if s == 0 else None
                                  ).astype(o_ref.dtype)
    y_ref[...] = y_ref[...] + _rms(down_ref[...], g3_ref[...])


def _sample_attend(pg, n_pages, db, b, pt_ref, ck_hbm, cv_hbm, clf_hbm, q, kn32, vn32, cn8,
                   kbuf, vbuf, lfbuf, ksem, vsem, lfsem,
                   qbd_ref, kcat_ref, vcat_ref, m_ref, l_ref, acc_ref, hooks):
    ng = n_pages // pg
    t = q.shape[0]
    nrow = t * H_A
    width = pg * PAGE

    def page_id(seq, j):
        return pt_ref[(n_pages - 1 - j) * db + seq]

    def kv_copies(pid, slot, i):
        return (pltpu.make_async_copy(ck_hbm.at[pid], kbuf.at[slot, i], ksem.at[slot]),
                pltpu.make_async_copy(cv_hbm.at[pid], vbuf.at[slot, i], vsem.at[slot]))

    def lf_copy(pid, lslot, j):
        return pltpu.make_async_copy(clf_hbm.at[pid], lfbuf.at[lslot, j], lfsem.at[lslot])

    def start_kv(seq, g, slot):
        for i in range(pg):
            for cp in kv_copies(page_id(seq, g * pg + i), slot, i):
                cp.start()

    def wait_kv(slot):
        for i in range(pg):
            for cp in kv_copies(0, slot, i):
                cp.wait()

    def start_lf(seq, lslot):
        for j in range(n_pages):
            lf_copy(page_id(seq, j), lslot, j).start()

    lslot = b % 2
    nslot = kbuf.shape[0]
    depth = nslot - 1

    @pl.when(b == 0)
    def _prime():
        start_lf(0, 0)
        for g0 in range(depth):
            start_kv(0, g0, g0)

    for j in range(n_pages):
        lf_copy(0, lslot, j).wait()

    @pl.when(b + 1 < db)
    def _next_lf():
        start_lf(b + 1, 1 - lslot)

    hsel = (lax.broadcasted_iota(jnp.int32, (H_A, MIX_A), 1) // HD_A
            == lax.broadcasted_iota(jnp.int32, (H_A, MIX_A), 0))
    qbd = jnp.where(hsel[None], q[:, None, :], 0.0)
    qbd_ref[...] = qbd.reshape(nrow, MIX_A).astype(bf16)

    run = jnp.zeros((H_A, LANES), f32)
    for g in range(ng):
        slot = g % nslot
        ahead = g + depth
        if ahead < ng:
            start_kv(b, ahead, ahead % nslot)
        else:
            @pl.when(b + 1 < db)
            def _next_seq():
                start_kv(b + 1, ahead - ng, ahead % nslot)
        wait_kv(slot)
        if hooks is not None:
            hooks[0](g)
        biases = []
        for i in range(pg):
            kcat_ref[:, i * PAGE:(i + 1) * PAGE] = kbuf[slot, i].reshape(MIX_A, PAGE).astype(bf16)
            vcat_ref[:, i * PAGE:(i + 1) * PAGE] = vbuf[slot, i].reshape(MIX_A, PAGE).astype(bf16)
            lf = lfbuf[lslot, g * pg + i]
            incl = _suffix_sums(lf)
            biases.append(incl - lf + run)
            run = run + incl[:, 0:1]
        bias = jnp.concatenate(biases, axis=1) * LOG2E
        s = _dot(qbd_ref[...], kcat_ref[...])
        if hooks is not None:
            hooks[1](g)
        s = (s.reshape(t, H_A, width) + bias[None]).reshape(nrow, width)
        m_cur = jnp.max(s, axis=1, keepdims=True)
        if g == 0:
            m_next = jnp.broadcast_to(m_cur, (nrow, LANES))
        else:
            m_prev = m_ref[...]
            m_next = jnp.maximum(m_prev, m_cur)
        p = jnp.exp2(s - _tile_lanes(m_next, width // LANES))
        p_sum = jnp.sum(p, axis=1, keepdims=True)
        pv = _dot_nt(p.astype(bf16), vcat_ref[...])
        if g == 0:
            l_ref[...] = jnp.broadcast_to(p_sum, (nrow, LANES))
            acc_ref[...] = pv
        else:
            a = jnp.exp2(m_prev - m_next)
            l_ref[...] = a * l_ref[...] + p_sum
            acc_ref[...] = acc_ref[...] * _tile_lanes(a, MIX_A // LANES) + pv
        m_ref[...] = m_next
    if hooks is not None:
        hooks[0](ng)

    pad = jnp.zeros((t, MIX_A), f32)
    kn = jnp.concatenate([kn32, pad], axis=0).astype(bf16)
    vn = jnp.concatenate([vn32, pad], axis=0).astype(bf16)
    cn = jnp.concatenate([cn8, jnp.zeros((H_A, t), f32)], axis=1) * LOG2E
    sn = _dot_nt(qbd_ref[...], kn)
    sn = (sn.reshape(t, H_A, 2 * t) - cn[None]).reshape(nrow, 2 * t)
    row_t = lax.broadcasted_iota(jnp.int32, (nrow, 2 * t), 0) // H_A
    col = lax.broadcasted_iota(jnp.int32, (nrow, 2 * t), 1)
    sn = jnp.where(col <= row_t, sn, NEG)
    m_prev = m_ref[...]
    m_next = jnp.maximum(m_prev, jnp.max(sn, axis=1, keepdims=True))
    a = jnp.exp2(m_prev - m_next)
    pn = jnp.exp2(sn - m_next[:, 0:1])
    l = a * l_ref[...] + jnp.sum(pn, axis=1, keepdims=True)
    acc = acc_ref[...] * _tile_lanes(a, MIX_A // LANES) + _dot(pn.astype(bf16), vn)
    o = acc / _tile_lanes(l, MIX_A // LANES)
    o3 = jnp.where(hsel[None], o.reshape(t, H_A, MIX_A), 0.0)
    return jnp.sum(o3, axis=1)


def _mlp_sample(x, oa, ob, wo, wu, wd, g1, g2, g3, pt_flat, ck, cv, clf, q, kn, vn, cnt, db, t, pg, tm):
    rows = x.shape[0]
    nsteps = rows // tm
    spp = db // nsteps
    n_pages = pt_flat.shape[0] // db
    ng = n_pages // pg
    assert db % nsteps == 0 and ng % KV_SLOTS == 0 and D_FF % ng == 0
    nrow = t * H_A
    row_spec = lambda width: pl.BlockSpec((tm, width), lambda i, pt: (i, 0))
    const = lambda shape: pl.BlockSpec(shape, lambda i, pt: (0, 0), pipeline_mode=pl.Buffered(1))
    seq_spec = lambda width: pl.BlockSpec((1, spp * t, width), lambda i, pt: (0, i, 0))
    hbm = pl.BlockSpec(memory_space=pl.ANY)
    grid_spec = pltpu.PrefetchScalarGridSpec(
        num_scalar_prefetch=1,
        grid=(nsteps,),
        in_specs=[row_spec(D_MODEL), row_spec(MIX_A), row_spec(MIX_B),
                  const((D_MODEL, D_MODEL)), const((D_MODEL, D_FF)), const((D_FF, D_MODEL)),
                  const((1, D_MODEL)), const((1, D_MODEL)), const((1, D_MODEL)),
                  hbm, hbm, hbm, seq_spec(MIX_A), seq_spec(MIX_A), seq_spec(MIX_A),
                  pl.BlockSpec((spp, H_A, t), lambda i, pt: (i, 0, 0))],
        out_specs=(row_spec(D_MODEL), pl.BlockSpec((spp, t, MIX_A), lambda i, pt: (i, 0, 0))),
        scratch_shapes=[
            pltpu.VMEM((KV_SLOTS, pg, H_A, HD_A, PAGE), f32),
            pltpu.VMEM((KV_SLOTS, pg, H_A, HD_A, PAGE), f32),
            pltpu.VMEM((2, n_pages, H_A, PAGE), f32),
            pltpu.SemaphoreType.DMA((KV_SLOTS,)),
            pltpu.SemaphoreType.DMA((KV_SLOTS,)),
            pltpu.SemaphoreType.DMA((2,)),
            pltpu.VMEM((nrow, MIX_A), bf16),
            pltpu.VMEM((MIX_A, pg * PAGE), bf16),
            pltpu.VMEM((MIX_A, pg * PAGE), bf16),
            pltpu.VMEM((nrow, LANES), f32),
            pltpu.VMEM((nrow, LANES), f32),
            pltpu.VMEM((nrow, MIX_A), f32),
            pltpu.VMEM((tm, D_MODEL), bf16),
            pltpu.VMEM((tm, D_FF // ng), bf16),
            pltpu.VMEM((tm, D_MODEL), f32),
        ],
    )
    return pl.pallas_call(
        functools.partial(_mlp_sample_kernel, pg, n_pages, db, spp),
        grid_spec=grid_spec,
        out_shape=(jax.ShapeDtypeStruct((rows, D_MODEL), f32),
                   jax.ShapeDtypeStruct((db, t, MIX_A), bf16)),
        compiler_params=_cparams(("arbitrary",)),
        name="mlp_sample",
    )(pt_flat, x, oa, ob, wo, wu, wd, g1, g2, g3, ck, cv, clf, q, kn, vn, cnt)


def _largest_divisor(n, candidates):
    for c in candidates:
        if n % c == 0:
            return c
    raise ValueError(f"no tile size for {n}")


def _rope_tables(pos):
    half = DK_B // 2
    inv = ROPE_BASE ** (-jnp.arange(half, dtype=f32) / half)
    ang = pos[:, None] * inv[None, :]
    cos = jnp.cos(ang)
    sin = jnp.sin(ang)
    return (jnp.concatenate([cos, cos, cos, cos], axis=1),
            jnp.concatenate([-sin, sin, -sin, sin], axis=1))


def _sum_matrix(seg_id, suffix):
    i = np.arange(seg_id.shape[0])
    same = seg_id[:, None] == seg_id[None, :]
    prefix = same & (i[None, :] <= i[:, None]) & ~suffix[:, None]
    later = same & (i[None, :] > i[:, None]) & suffix[:, None]
    return jnp.asarray(prefix.astype(np.float32) - later.astype(np.float32), bf16)


def kernel(x_prompt, x_sample, cache_k, cache_v, cache_logf, state_ret, page_table, meta_tokens,
           g_pre_mix, w_in, b_f, w_out, g_post_mix, g_pre_mlp, w_up, w_down, g_post_mlp):
    nb, seq, _ = x_prompt.shape
    db, t, _ = x_sample.shape
    n_pool = cache_k.shape[1]
    n_pages = page_table.shape[1]
    n_s = db * t
    assert w_in.shape[0] == 1, "single layer"
    assert seq % 256 == 0 and n_s % LANES == 0 and cache_k.shape[2] == PAGE and n_pages % 16 == 0

    w = w_in[0]
    n_fa = 3 * MIX_A
    w_p = jnp.concatenate([w[:, :n_fa], w[:, n_fa + H_A:], w[:, n_fa:n_fa + H_A],
                           jnp.zeros((D_MODEL, D_INP - w.shape[1]), w.dtype)], axis=1).astype(bf16)
    bf_p = jnp.concatenate([b_f[0], jnp.zeros((LANES - H_A,), f32)])[None]
    g0 = g_pre_mix[0][None]
    wo = w_out[0].astype(bf16)
    wu = w_up[0].astype(bf16)
    wd = w_down[0].astype(bf16)
    g1, g2, g3 = g_post_mix[0][None], g_pre_mlp[0][None], g_post_mlp[0][None]

    rows_aux = n_s + LANES
    n_pad = LANES - N_META
    x_aux = jnp.concatenate([x_sample.reshape(n_s, D_MODEL), meta_tokens,
                             jnp.zeros((n_pad, D_MODEL), f32)], axis=0)[None]
    past = n_pages * PAGE
    pos_aux = jnp.concatenate([jnp.tile(past + jnp.arange(t, dtype=f32), db),
                               jnp.arange(N_META, dtype=f32), jnp.zeros((n_pad,), f32)])
    cos_a, sin_a = _rope_tables(pos_aux)
    seg_aux = np.concatenate([np.arange(n_s) // t, np.full((N_META,), db), np.full((n_pad,), db + 1)])
    is_meta = seg_aux == db
    (q_a, kaug_a, vt_a, k32_a, v32_a, lf_a, ccol_a, qr_a, kr_a, vr_a, gr_a) = _proj(
        False, x_aux, g0, w_p, bf_p, cos_a, sin_a, _sum_matrix(seg_aux, is_meta), rows_aux)
    meta_blk16 = n_s // N_META
    meta_blk128 = n_s // LANES

    tm = _largest_divisor(seq, (512, 256))
    cos_p, sin_p = _rope_tables(N_META + jnp.arange(seq, dtype=f32))
    one_seg = np.zeros((tm,), np.int64)
    (qaug_p, kaug_p, vt_p, k32_p, v32_p, lf_p, qr_p, kr_p, vr_p, gr_p) = _proj(
        True, x_prompt, g0, w_p, bf_p, cos_p, sin_p, _sum_matrix(one_seg, one_seg > 0), tm,
        meta=(k32_a, v32_a, lf_a, meta_blk16))

    oa_p = _fox_prompt(qaug_p, kaug_p, vt_p, kaug_a, vt_a, meta_blk128, min(tm, Q_TILE))
    ob_p, s_p = _ret_prompt(qr_p, kr_p, vr_p, gr_p, kr_a, vr_a, meta_blk16, 256)

    cnt = ccol_a[0, :n_s].reshape(db, t, H_A).transpose(0, 2, 1)
    pg = KV_GROUP
    tm_mlp = 256
    y_p, oa_s = _mlp_sample(x_prompt.reshape(nb * seq, D_MODEL), oa_p.reshape(nb * seq, MIX_A),
                            ob_p.reshape(nb * seq, MIX_B), wo, wu, wd, g1, g2, g3,
                            page_table.T.reshape(n_pages * db),
                            cache_k[0].transpose(0, 2, 3, 1), cache_v[0].transpose(0, 2, 3, 1),
                            cache_logf[0].transpose(0, 2, 1),
                            q_a, k32_a, v32_a, cnt, db, t, pg, tm_mlp)

    sb = _largest_divisor(db, (8, 4, 2))
    ob_s, s_s = _ret_sample(qr_a, kr_a, vr_a, gr_a, state_ret[0], db, t, sb)
    tms = _largest_divisor(n_s, (512, 256, 128))
    y_s = _mlp(x_sample.reshape(n_s, D_MODEL), oa_s.reshape(n_s, MIX_A), ob_s[0],
               wo, wu, wd, g1, g2, g3, tms)

    k_prompt = k32_p.reshape(1, nb, N_META + seq, H_A, HD_A)
    v_prompt = v32_p.reshape(1, nb, N_META + seq, H_A, HD_A)
    return (y_p.reshape(nb, seq, D_MODEL), y_s.reshape(db, t, D_MODEL),
            k_prompt, v_prompt, lf_p[None], s_p[None],
            k32_a[0, :n_s].reshape(1, db, t, H_A, HD_A), v32_a[0, :n_s].reshape(1, db, t, H_A, HD_A),
            lf_a[0, :n_s].reshape(1, db, t, H_A), s_s[None])
```

```python
import functools

import numpy as np
import jax
import jax.numpy as jnp
from jax import lax
from jax.experimental import pallas as pl
from jax.experimental.pallas import tpu as pltpu

f32 = jnp.float32
bf16 = jnp.bfloat16

D_MODEL = 1024
N_META = 16
PAGE = 128
HD_A = 64
H_A = 8
H_B = 4
DK_B = 64
DV_B = 128
MIX_A = H_A * HD_A
MIX_B = H_B * DV_B
D_FF = 4 * D_MODEL
ROPE_BASE = 10000.0
EPS = 1e-6
NEG = -1e30
LOG2E = 1.4426950408889634

QA, KA, VA, QR, KR, VR, GR, FA = 0, 512, 1024, 1536, 1792, 2048, 2560, 3072
D_INP = 3200
LANES = 128
SUBLANES = 8
AUG = H_A * LANES
ONE_LANE = 3 * H_A
VT_ROWS = 80
SCORES_AHEAD = 3
KV_TILE = 1024
Q_TILE = 512
KV_GROUP = 8
KV_SLOTS = 4

VMEM_LIMIT = 56 * 1024 * 1024


def _cparams(sem):
    return pltpu.CompilerParams(dimension_semantics=sem, vmem_limit_bytes=VMEM_LIMIT)


def _dot(a, b):
    return jnp.dot(a, b, preferred_element_type=f32)


def _dot_nt(a, b):
    return lax.dot_general(a, b, (((1,), (1,)), ((), ())), preferred_element_type=f32)


def _dot_tn(a, b):
    return lax.dot_general(a, b, (((0,), (0,)), ((), ())), preferred_element_type=f32)


def _split3(x):
    hi = x.astype(bf16).astype(f32)
    r = x - hi
    mid = r.astype(bf16).astype(f32)
    lo = (r - mid).astype(bf16).astype(f32)
    return hi, mid, lo


def _pack3(x, lane):
    hi, mid, lo = _split3(x)
    return (hi + pltpu.roll(mid, H_A, 1) + pltpu.roll(lo, 2 * H_A, 1)
            + jnp.where(lane == ONE_LANE, 1.0, 0.0))


def _rms(x, g):
    return x * lax.rsqrt(jnp.mean(x * x, axis=-1, keepdims=True) + EPS) * g


def _own_lanes(g, lane):
    return (lane < HD_A) if g % 2 == 0 else (lane >= HD_A)


def _placements():
    rk = np.zeros((LANES, AUG), np.float32)
    rq = np.zeros((LANES, AUG), np.float32)
    for g in range(H_A):
        base = g * LANES + (HD_A if g % 2 == 0 else 0)
        for part in range(3):
            rk[part * H_A + g, base + part] = 1.0
            rk[ONE_LANE, base + 3 + part] = 1.0
            rq[ONE_LANE, base + part] = 1.0
            rq[part * H_A + g, base + 3 + part] = 1.0
    return jnp.asarray(rk, bf16), jnp.asarray(rq, bf16)


def _proj_kernel(prompt, x_ref, g_ref, w_ref, bf_ref, cos_ref, sin_ref, a_ref, rk_ref, rq_ref, *refs):
    if prompt:
        (km_ref, vm_ref, lfm_ref, qaug_ref, kaug_ref, vt_ref, k_hbm, v_hbm, lf_hbm,
         qr_ref, kr_ref, vr_ref, gr_ref, carry_ref, kst_ref, vst_ref, lst_ref, osem, msem) = refs
    else:
        (q_ref, kaug_ref, vt_ref, k32_ref, v32_ref, lf_ref, ccol_ref,
         qr_ref, kr_ref, vr_ref, gr_ref, carry_ref) = refs
    t = pl.program_id(1)
    x = x_ref[0]
    h = _rms(x, g_ref[...]).astype(bf16)
    z = _dot(h, w_ref[...])
    tm = z.shape[0]
    lane = lax.broadcasted_iota(jnp.int32, (tm, LANES), 1)

    qs = z[:, QA:QA + MIX_A] * (HD_A ** -0.5 * LOG2E)
    k = z[:, KA:KA + MIX_A]
    v = z[:, VA:VA + MIX_A]
    if not prompt:
        k32_ref[0] = k
        v32_ref[0] = v
    vt3 = v.T.reshape(H_A, HD_A, tm)
    extra = jnp.where(lax.broadcasted_iota(jnp.int32, (H_A, VT_ROWS - HD_A, tm), 1) == 0, 1.0, 0.0)
    vt_ref[0] = jnp.concatenate([vt3, extra], axis=1).reshape(H_A * VT_ROWS, tm).astype(bf16)
    vr_ref[0] = z[:, VR:VR + MIX_B].astype(bf16)
    gr_ref[0] = z[:, GR:GR + MIX_B]

    cos = cos_ref[...]
    sin = sin_ref[...]
    first_half = (lane % DK_B) < (DK_B // 2)
    for s in range(4):
        zs = z[:, QR + s * LANES: QR + (s + 1) * LANES]
        partner = jnp.where(first_half, pltpu.roll(zs, LANES - DK_B // 2, 1),
                            pltpu.roll(zs, DK_B // 2, 1))
        r = zs * cos + partner * sin
        if s < 2:
            qr_ref[0, :, s * LANES:(s + 1) * LANES] = r
        else:
            kr_ref[0, :, (s - 2) * LANES:(s - 1) * LANES] = r * (DK_B ** -0.5)

    fa = z[:, FA:FA + LANES] + bf_ref[...]
    lf = jnp.where(lane < H_A, jnp.minimum(fa, 0.0) - jnp.log1p(jnp.exp(-jnp.abs(fa))), 0.0)
    if prompt:
        nt = pl.num_programs(1)
        step = pl.program_id(0) * nt + t
        slot = step % 2

        def out_copies(slot_, b_, t_):
            rows = pl.ds(N_META + t_ * tm, tm)
            return (pltpu.make_async_copy(kst_ref.at[slot_], k_hbm.at[b_, rows], osem.at[0, slot_]),
                    pltpu.make_async_copy(vst_ref.at[slot_], v_hbm.at[b_, rows], osem.at[1, slot_]),
                    pltpu.make_async_copy(lst_ref.at[slot_], lf_hbm.at[b_, rows], osem.at[2, slot_]))

        @pl.when(step >= 2)
        def _reuse():
            for cp in out_copies(slot, 0, 0):
                cp.wait()

        kst_ref[slot] = k
        vst_ref[slot] = v
        lst_ref[slot] = lf[:, :H_A]
        for cp in out_copies(slot, pl.program_id(0), t):
            cp.start()

        @pl.when(t == 0)
        def _meta_rows():
            head = pl.ds(0, N_META)
            cps = (pltpu.make_async_copy(km_ref.at[0], k_hbm.at[pl.program_id(0), head], msem.at[0]),
                   pltpu.make_async_copy(vm_ref.at[0], v_hbm.at[pl.program_id(0), head], msem.at[1]),
                   pltpu.make_async_copy(lfm_ref.at[0], lf_hbm.at[pl.program_id(0), head], msem.at[2]))
            for cp in cps:
                cp.start()
            for cp in cps:
                cp.wait()

        @pl.when(step == pl.num_programs(0) * nt - 1)
        def _drain():
            for cp in out_copies(slot, 0, 0) + out_copies(1 - slot, 0, 0):
                cp.wait()
    else:
        lf_ref[0] = lf[:, :H_A]
    hi, mid, lo = _split3(lf)
    parts = (hi + pltpu.roll(mid, H_A, 1) + pltpu.roll(lo, 2 * H_A, 1)).astype(bf16)
    cs = _dot(a_ref[...], parts)
    c = cs + pltpu.roll(cs, LANES - H_A, 1) + pltpu.roll(cs, LANES - 2 * H_A, 1)

    @pl.when(t == 0)
    def _():
        carry_ref[...] = jnp.zeros_like(carry_ref)

    carry = carry_ref[...]
    c = jnp.where(lane < H_A, c + carry[0:1], 0.0)
    carry_ref[...] = jnp.broadcast_to(c[tm - 1:tm], carry_ref.shape)

    e_k = _dot(_pack3(-LOG2E * c, lane).astype(bf16), rk_ref[...])
    for g in range(H_A):
        pair = k[:, (g // 2) * LANES:(g // 2 + 1) * LANES]
        kaug_ref[0, :, g * LANES:(g + 1) * LANES] = jnp.where(
            _own_lanes(g, lane), pair, e_k[:, g * LANES:(g + 1) * LANES]).astype(bf16)
    if prompt:
        lane8 = lane[:SUBLANES]
        qx = _pack3(LOG2E * carry, lane8)
        e_q = _dot(jnp.concatenate([qx, qx], axis=0).astype(bf16), rq_ref[...])
        for g in range(H_A):
            pair = qs[:, (g // 2) * LANES:(g // 2 + 1) * LANES]
            qaug_ref[0, :, g * LANES:(g + 1) * LANES] = jnp.where(
                _own_lanes(g, lane), pair, e_q[0:1, g * LANES:(g + 1) * LANES]).astype(bf16)
    else:
        q_ref[0] = qs
        ccol_ref[0] = c[:, :H_A]


def _proj(prompt, x, g, w, bfp, cos_t, sin_t, a, tm, meta=None):
    nb, rows, _ = x.shape
    nt = rows // tm
    rk, rq = _placements()
    row_spec = lambda width: pl.BlockSpec((1, tm, width), lambda b, t: (b, t, 0))
    const = lambda shape: pl.BlockSpec(shape, lambda b, t: (0,) * len(shape))
    rows_of = lambda width, dt: jax.ShapeDtypeStruct((nb, rows, width), dt)
    vt_shape = jax.ShapeDtypeStruct((nb, H_A * VT_ROWS, rows), bf16)
    vt_spec = pl.BlockSpec((1, H_A * VT_ROWS, tm), lambda b, t: (b, 0, t))
    tail_shape = (rows_of(H_B * DK_B, f32), rows_of(H_B * DK_B, f32), rows_of(MIX_B, bf16), rows_of(MIX_B, f32))
    tail_spec = (row_spec(H_B * DK_B), row_spec(H_B * DK_B), row_spec(MIX_B), row_spec(MIX_B))
    extra_in, extra_specs = (), []
    scratch = [pltpu.VMEM((SUBLANES, LANES), f32)]
    if prompt:
        assert nb * nt >= 2
        k_a, v_a, lf_a, meta_blk = meta
        full = lambda width: jax.ShapeDtypeStruct((nb, N_META + rows, width), f32)
        hbm = pl.BlockSpec(memory_space=pl.ANY)
        out_shape = (rows_of(AUG, bf16), rows_of(AUG, bf16), vt_shape, full(MIX_A), full(MIX_A),
                     full(H_A)) + tail_shape
        out_specs = (row_spec(AUG), row_spec(AUG), vt_spec, hbm, hbm, hbm) + tail_spec
        meta_spec = lambda width: pl.BlockSpec((1, N_META, width), lambda b, t: (0, meta_blk, 0))
        extra_in = (k_a, v_a, lf_a)
        extra_specs = [meta_spec(MIX_A), meta_spec(MIX_A), meta_spec(H_A)]
        scratch += [pltpu.VMEM((2, tm, MIX_A), f32), pltpu.VMEM((2, tm, MIX_A), f32),
                    pltpu.VMEM((2, tm, H_A), f32),
                    pltpu.SemaphoreType.DMA((3, 2)), pltpu.SemaphoreType.DMA((3,))]
    else:
        out_shape = (rows_of(MIX_A, f32), rows_of(AUG, bf16), vt_shape, rows_of(MIX_A, f32), rows_of(MIX_A, f32),
                     rows_of(H_A, f32), rows_of(H_A, f32)) + tail_shape
        out_specs = (row_spec(MIX_A), row_spec(AUG), vt_spec, row_spec(MIX_A), row_spec(MIX_A),
                     row_spec(H_A), row_spec(H_A)) + tail_spec
    return pl.pallas_call(
        functools.partial(_proj_kernel, prompt),
        grid=(nb, nt),
        in_specs=[
            row_spec(D_MODEL),
            const((1, D_MODEL)),
            const((D_MODEL, D_INP)),
            const((1, LANES)),
            pl.BlockSpec((tm, LANES), lambda b, t: (t, 0)),
            pl.BlockSpec((tm, LANES), lambda b, t: (t, 0)),
            const((tm, tm)),
            const((LANES, AUG)),
            const((LANES, AUG)),
        ] + extra_specs,
        out_specs=out_specs,
        out_shape=out_shape,
        scratch_shapes=scratch,
        compiler_params=_cparams(("arbitrary", "arbitrary")),
        name="proj_prompt" if prompt else "proj_aux",
    )(x, g, w, bfp, cos_t, sin_t, a, rk, rq, *extra_in)


def _fox_prompt_kernel(r, qi_tab, kj_tab, mode_tab, q_ref, k_ref, vt_ref, km_ref, vtm_ref, o_ref,
                       m_ref, acc_ref):
    p_id = pl.program_id(1)
    kj = kj_tab[p_id]
    mode = mode_tab[p_id]
    tq = q_ref.shape[1]

    def attend(kk_ref, vv_ref, keys, keep, first):
        ks = keys.stop - keys.start
        m_new, scale, pvs = [], [], []

        def scores(h):
            cols = slice(h * LANES, (h + 1) * LANES)
            return _dot_nt(kk_ref[0, keys, cols], q_ref[0, :, cols])

        ahead = [scores(h) for h in range(SCORES_AHEAD)]
        for h in range(H_A):
            st = ahead.pop(0)
            if h + SCORES_AHEAD < H_A:
                ahead.append(scores(h + SCORES_AHEAD))
            if keep is not None:
                st = jnp.where(keep, st, NEG)
            s3 = st.reshape(ks // SUBLANES, SUBLANES, tq)
            m_cur = jnp.max(jnp.max(s3, axis=0), axis=0, keepdims=True)
            if first:
                m_next = jnp.broadcast_to(m_cur, (SUBLANES, tq))
            else:
                m_prev = m_ref[h]
                m_next = jnp.maximum(m_prev, m_cur)
                a = jnp.exp2(m_prev - m_next)
                scale.append(jnp.broadcast_to(a[None], (VT_ROWS // SUBLANES, SUBLANES, tq)).reshape(VT_ROWS, tq))
            p = jnp.exp2(s3 - m_next[None]).reshape(ks, tq).astype(bf16)
            pvs.append(_dot(vv_ref[0, h * VT_ROWS:(h + 1) * VT_ROWS, keys], p))
            m_new.append(m_next)
        m_ref[...] = jnp.stack(m_new)
        pv = jnp.concatenate(pvs, axis=0)
        if first:
            acc_ref[...] = pv
        else:
            acc_ref[...] = acc_ref[...] * jnp.concatenate(scale, axis=0) + pv

    block = lambda s: slice(s * tq, (s + 1) * tq)

    @pl.when(kj == 0)
    def _meta():
        nk = km_ref.shape[1]
        attend(km_ref, vtm_ref, slice(0, nk), lax.broadcasted_iota(jnp.int32, (nk, tq), 0) < N_META, True)

    for s in range(r):
        @pl.when((mode == 0) | (mode > s + 1))
        def _before():
            attend(k_ref, vt_ref, block(s), None, False)

    causal = (lax.broadcasted_iota(jnp.int32, (tq, tq), 0) <= lax.broadcasted_iota(jnp.int32, (tq, tq), 1))
    for d in range(1, r + 1):
        @pl.when(mode == d)
        def _last():
            attend(k_ref, vt_ref, block(d - 1), causal, False)
            acc3 = acc_ref[...].reshape(H_A, VT_ROWS, tq)
            o = acc3[:, :HD_A] / acc3[:, HD_A:HD_A + 1]
            o_ref[0] = o.reshape(MIX_A, tq).T.astype(o_ref.dtype)


def _fox_prompt(q_aug, k_aug, vt, km_src, vtm_src, meta_blk, tq):
    nb, rows, _ = q_aug.shape
    nq = rows // tq
    r = max(d for d in (1, 2, 4) if nq % d == 0 and d * tq <= KV_TILE)
    tk = r * tq
    qi_np, kj_np, mode_np = [], [], []
    for i in range(nq):
        for j in range(i // r + 1):
            qi_np.append(i)
            kj_np.append(j)
            mode_np.append(i % r + 1 if j == i // r else 0)
    tabs = [jnp.asarray(np.array(a, np.int32)) for a in (qi_np, kj_np, mode_np)]
    vrows = H_A * VT_ROWS
    grid_spec = pltpu.PrefetchScalarGridSpec(
        num_scalar_prefetch=3,
        grid=(nb, len(qi_np)),
        in_specs=[
            pl.BlockSpec((1, tq, AUG), lambda b, p, qt, kt, mt: (b, qt[p], 0)),
            pl.BlockSpec((1, tk, AUG), lambda b, p, qt, kt, mt: (b, kt[p], 0)),
            pl.BlockSpec((1, vrows, tk), lambda b, p, qt, kt, mt: (b, 0, kt[p])),
            pl.BlockSpec((1, LANES, AUG), lambda b, p, qt, kt, mt: (0, meta_blk, 0)),
            pl.BlockSpec((1, vrows, LANES), lambda b, p, qt, kt, mt: (0, 0, meta_blk)),
        ],
        out_specs=pl.BlockSpec((1, tq, MIX_A), lambda b, p, qt, kt, mt: (b, qt[p], 0)),
        scratch_shapes=[
            pltpu.VMEM((H_A, SUBLANES, tq), f32),
            pltpu.VMEM((vrows, tq), f32),
        ],
    )
    return pl.pallas_call(
        functools.partial(_fox_prompt_kernel, r),
        grid_spec=grid_spec,
        out_shape=jax.ShapeDtypeStruct((nb, rows, MIX_A), bf16),
        compiler_params=_cparams(("arbitrary", "arbitrary")),
        name="fox_prompt",
    )(*tabs, q_aug, k_aug, vt, km_src, vtm_src)


def _log_gamma():
    return np.log1p(-np.exp2(-5.0 - np.arange(H_B, dtype=np.float64)))


def _ret_tables(c):
    lg = _log_gamma()
    n = np.arange(c, dtype=np.float64)
    rel = n[:, None] - n[None, :]
    dec = np.where(rel >= 0, np.exp(np.maximum(rel, 0.0)[None] * lg[:, None, None]), 0.0)
    q_dec = np.exp((n + 1.0)[None, :] * lg[:, None])
    k_dec = np.exp((c - 1.0 - n)[None, :] * lg[:, None])
    q_dec_full = np.broadcast_to(q_dec[:, :, None], (H_B, c, DV_B))
    k_dec_full = np.repeat(k_dec.T, DK_B, axis=1)
    g_c = [float(v) for v in np.exp(c * lg)]
    return (jnp.asarray(dec, f32), jnp.asarray(q_dec_full, f32), jnp.asarray(k_dec_full, f32), g_c)


def _head_norm_gate(o, gate):
    mu = jnp.mean(o, axis=-1, keepdims=True)
    d = o - mu
    var = jnp.mean(d * d, axis=-1, keepdims=True)
    y = d * lax.rsqrt(var + EPS)
    return y * (gate * jax.nn.sigmoid(gate))


def _ret_prompt_kernel(g_c, qr_ref, kr_ref, vr_ref, gr_ref, dec_ref, qdec_ref, kdec_ref,
                       krm_ref, vrm_ref, kdecm_ref, ob_ref, s_out_ref, s_ref):
    c = pl.program_id(1)

    @pl.when(c == 0)
    def _init():
        kd = (krm_ref[0] * kdecm_ref[...]).astype(bf16)
        for h in range(H_B):
            s_ref[h] = _dot_tn(kd[:, h * DK_B:(h + 1) * DK_B], vrm_ref[0, :, h * DV_B:(h + 1) * DV_B])

    q = qr_ref[0].astype(bf16)
    k = kr_ref[0]
    kb = k.astype(bf16)
    kd = (k * kdec_ref[...]).astype(bf16)
    heads = range(H_B)
    dk = lambda h: slice(h * DK_B, (h + 1) * DK_B)
    dv = lambda h: slice(h * DV_B, (h + 1) * DV_B)
    s_old = [s_ref[h] for h in heads]
    inner = [_dot_nt(q[:, dk(h)], kb[:, dk(h)]) for h in heads]
    cross = [_dot(q[:, dk(h)], s_old[h].astype(bf16)) for h in heads]
    s_ref[...] = jnp.stack([g_c[h] * s_old[h] + _dot_tn(kd[:, dk(h)], vr_ref[0, :, dv(h)]) for h in heads])
    for h in heads:
        o = _dot((inner[h] * dec_ref[h]).astype(bf16), vr_ref[0, :, dv(h)]) + cross[h] * qdec_ref[h]
        ob_ref[0, :, dv(h)] = _head_norm_gate(o, gr_ref[0, :, dv(h)]).astype(ob_ref.dtype)

    @pl.when(c == pl.num_programs(1) - 1)
    def _fin():
        s_out_ref[0] = s_ref[...]


def _ret_prompt(qr, kr, vr, gr, krm_src, vrm_src, meta_blk, chunk):
    nb, rows, _ = qr.shape
    nc = rows // chunk
    dec, qdec, kdec, g_c = _ret_tables(chunk)
    _, _, kdec_m, _ = _ret_tables(N_META)
    row_spec = lambda width: pl.BlockSpec((1, chunk, width), lambda b, c: (b, c, 0))
    const = lambda shape: pl.BlockSpec(shape, lambda b, c: (0,) * len(shape))
    return pl.pallas_call(
        functools.partial(_ret_prompt_kernel, g_c),
        grid=(nb, nc),
        in_specs=[
            row_spec(H_B * DK_B), row_spec(H_B * DK_B), row_spec(MIX_B), row_spec(MIX_B),
            const((H_B, chunk, chunk)), const((H_B, chunk, DV_B)), const((chunk, H_B * DK_B)),
            pl.BlockSpec((1, N_META, H_B * DK_B), lambda b, c: (0, meta_blk, 0)),
            pl.BlockSpec((1, N_META, MIX_B), lambda b, c: (0, meta_blk, 0)),
            const((N_META, H_B * DK_B)),
        ],
        out_specs=(row_spec(MIX_B),
                   pl.BlockSpec((1, H_B, DK_B, DV_B), lambda b, c: (b, 0, 0, 0))),
        out_shape=(jax.ShapeDtypeStruct((nb, rows, MIX_B), bf16),
                   jax.ShapeDtypeStruct((nb, H_B, DK_B, DV_B), f32)),
        scratch_shapes=[pltpu.VMEM((H_B, DK_B, DV_B), f32)],
        compiler_params=_cparams(("arbitrary", "arbitrary")),
        name="ret_prompt",
    )(qr, kr, vr, gr, dec, qdec, kdec, krm_src, vrm_src, kdec_m)


def _ret_sample_kernel(g_c, sb, qr_ref, kr_ref, vr_ref, gr_ref, st_ref, dec_ref, qdec_ref, kdec_ref,
                       ob_ref, s_out_ref):
    t = qr_ref.shape[1] // sb
    q = qr_ref[0].reshape(sb, t, H_B * DK_B)
    k = kr_ref[0].reshape(sb, t, H_B * DK_B)
    kd = k * kdec_ref[...][None]
    v = vr_ref[0].astype(f32).reshape(sb, t, MIX_B)
    gate = gr_ref[0].reshape(sb, t, MIX_B)
    for h in range(H_B):
        qh = q[:, :, h * DK_B:(h + 1) * DK_B]
        kh = k[:, :, h * DK_B:(h + 1) * DK_B]
        kdh = kd[:, :, h * DK_B:(h + 1) * DK_B]
        vh = v[:, :, h * DV_B:(h + 1) * DV_B]
        s_old = st_ref[:, h]
        inner = jnp.einsum('btd,bsd->bts', qh, kh, preferred_element_type=f32) * dec_ref[h][None]
        o = (jnp.einsum('bts,bse->bte', inner, vh, preferred_element_type=f32)
             + jnp.einsum('btd,bde->bte', qh, s_old, preferred_element_type=f32)
             * qdec_ref[h][None])
        s_out_ref[:, h] = g_c[h] * s_old + jnp.einsum('btd,bte->bde', kdh, vh,
                                                      preferred_element_type=f32)
        y = _head_norm_gate(o, gate[:, :, h * DV_B:(h + 1) * DV_B])
        ob_ref[0, :, h * DV_B:(h + 1) * DV_B] = y.reshape(sb * t, DV_B).astype(ob_ref.dtype)


def _ret_sample(qr, kr, vr, gr, state, db, t, sb):
    dec, qdec, kdec, g_c = _ret_tables(t)
    nsteps = db // sb
    row_spec = lambda width: pl.BlockSpec((1, sb * t, width), lambda i: (0, i, 0))
    const = lambda shape: pl.BlockSpec(shape, lambda i: (0,) * len(shape))
    st_spec = pl.BlockSpec((sb, H_B, DK_B, DV_B), lambda i: (i, 0, 0, 0))
    return pl.pallas_call(
        functools.partial(_ret_sample_kernel, g_c, sb),
        grid=(nsteps,),
        in_specs=[row_spec(H_B * DK_B), row_spec(H_B * DK_B), row_spec(MIX_B), row_spec(MIX_B),
                  st_spec, const((H_B, t, t)), const((H_B, t, DV_B)), const((t, H_B * DK_B))],
        out_specs=(pl.BlockSpec((1, sb * t, MIX_B), lambda i: (0, i, 0)), st_spec),
        out_shape=(jax.ShapeDtypeStruct((1, db * t, MIX_B), bf16),
                   jax.ShapeDtypeStruct((db, H_B, DK_B, DV_B), f32)),
        compiler_params=_cparams(("arbitrary",)),
        name="ret_sample",
    )(qr, kr, vr, gr, state, dec, qdec, kdec)


def _mlp_kernel(x_ref, oa_ref, ob_ref, wo_ref, wu_ref, wd_ref, g1_ref, g2_ref, g3_ref, y_ref):
    mixed = _dot(oa_ref[...], wo_ref[0:MIX_A, :]) + _dot(ob_ref[...], wo_ref[MIX_A:, :])
    x1 = x_ref[...] + _rms(mixed, g1_ref[...])
    hn = _rms(x1, g2_ref[...]).astype(bf16)
    u = jnp.square(jnp.maximum(_dot(hn, wu_ref[...]), 0.0)).astype(bf16)
    y_ref[...] = x1 + _rms(_dot(u, wd_ref[...]), g3_ref[...])


def _mlp(x, oa, ob, wo, wu, wd, g1, g2, g3, tm):
    rows = x.shape[0]
    row_spec = lambda width: pl.BlockSpec((tm, width), lambda i: (i, 0))
    const = lambda shape: pl.BlockSpec(shape, lambda i: (0, 0), pipeline_mode=pl.Buffered(1))
    return pl.pallas_call(
        _mlp_kernel,
        grid=(rows // tm,),
        in_specs=[row_spec(D_MODEL), row_spec(MIX_A), row_spec(MIX_B),
                  const((D_MODEL, D_MODEL)), const((D_MODEL, D_FF)), const((D_FF, D_MODEL)),
                  const((1, D_MODEL)), const((1, D_MODEL)), const((1, D_MODEL))],
        out_specs=row_spec(D_MODEL),
        out_shape=jax.ShapeDtypeStruct((rows, D_MODEL), f32),
        compiler_params=_cparams(("arbitrary",)),
        name="merge_mlp",
    )(x, oa, ob, wo, wu, wd, g1, g2, g3)


def _tile_lanes(x, n):
    return jnp.concatenate([x] * n, axis=1)


def _suffix_sums(x):
    lane = lax.broadcasted_iota(jnp.int32, x.shape, 1)
    s = x
    k = 1
    while k < LANES:
        s = s + jnp.where(lane + k < LANES, pltpu.roll(s, LANES - k, 1), 0.0)
        k *= 2
    return s


def _mlp_sample_kernel(pg, n_pages, db, spp, pt_ref,
                       x_ref, oa_ref, ob_ref, wo_ref, wu_ref, wd_ref, g1_ref, g2_ref, g3_ref,
                       ck_hbm, cv_hbm, clf_hbm, q_ref, kn_ref, vn_ref, cn_ref,
                       y_ref, o_ref, kbuf, vbuf, lfbuf, ksem, vsem, lfsem,
                       qbd_ref, m_ref, l_ref, acc_ref, hn_ref, u_ref, down_ref):
    ng = n_pages // pg
    t = q_ref.shape[1] // spp
    ff = D_FF // ng

    def mlp_before_scores(g):
        if g == 0:
            mixed = _dot(oa_ref[...], wo_ref[0:MIX_A, :]) + _dot(ob_ref[...], wo_ref[MIX_A:, :])
            x1 = x_ref[...] + _rms(mixed, g1_ref[...])
            y_ref[...] = x1
            hn_ref[...] = _rms(x1, g2_ref[...]).astype(bf16)
        else:
            d = _dot(u_ref[...], wd_ref[(g - 1) * ff:g * ff, :])
            if g == 1:
                down_ref[...] = d
            else:
                down_ref[...] = down_ref[...] + d

    def mlp_before_softmax(g):
        u = _dot(hn_ref[...], wu_ref[:, g * ff:(g + 1) * ff])
        u_ref[...] = jnp.square(jnp.maximum(u, 0.0)).astype(bf16)

    for s in range(spp):
        seq = pl.program_id(0) * spp + s
        o_ref[s] = _sample_attend(pg, n_pages, db, seq, pt_ref, ck_hbm, cv_hbm, clf_hbm,
                                  q_ref[0, s * t:(s + 1) * t], kn_ref[0, s * t:(s + 1) * t],
                                  vn_ref[0, s * t:(s + 1) * t], cn_ref[s],
                                  kbuf, vbuf, lfbuf, ksem, vsem, lfsem,
                                  qbd_ref, m_ref, l_ref, acc_ref,
                                  (mlp_before_scores, mlp_before_softmax) if s == 0 else None
                                  ).astype(o_ref.dtype)
    y_ref[...] = y_ref[...] + _rms(down_ref[...], g3_ref[...])


def _sample_attend(pg, n_pages, db, b, pt_ref, ck_hbm, cv_hbm, clf_hbm, q, kn32, vn32, cn8,
                   kbuf, vbuf, lfbuf, ksem, vsem, lfsem,
                   qbd_ref, m_ref, l_ref, acc_ref, hooks):
    ng = n_pages // pg
    t = q.shape[0]
    nrow = t * H_A
    width = pg * PAGE

    def page_id(seq, j):
        return pt_ref[(n_pages - 1 - j) * db + seq]

    def kv_copies(pid, slot, i):
        lanes = pl.ds(i * PAGE, PAGE)
        return (pltpu.make_async_copy(ck_hbm.at[pid], kbuf.at[slot, :, :, lanes], ksem.at[slot]),
                pltpu.make_async_copy(cv_hbm.at[pid], vbuf.at[slot, :, :, lanes], vsem.at[slot]))

    def lf_copy(pid, lslot, j):
        return pltpu.make_async_copy(clf_hbm.at[pid], lfbuf.at[lslot, j], lfsem.at[lslot])

    def start_kv(seq, g, slot):
        for i in range(pg):
            for cp in kv_copies(page_id(seq, g * pg + i), slot, i):
                cp.start()

    def wait_kv(slot):
        for i in range(pg):
            for cp in kv_copies(0, slot, i):
                cp.wait()

    def start_lf(seq, lslot):
        for j in range(n_pages):
            lf_copy(page_id(seq, j), lslot, j).start()

    lslot = b % 2
    nslot = kbuf.shape[0]
    depth = nslot - 1

    @pl.when(b == 0)
    def _prime():
        start_lf(0, 0)
        for g0 in range(depth):
            start_kv(0, g0, g0)

    for j in range(n_pages):
        lf_copy(0, lslot, j).wait()

    @pl.when(b + 1 < db)
    def _next_lf():
        start_lf(b + 1, 1 - lslot)

    hsel = (lax.broadcasted_iota(jnp.int32, (H_A, MIX_A), 1) // HD_A
            == lax.broadcasted_iota(jnp.int32, (H_A, MIX_A), 0))
    qbd = jnp.where(hsel[None], q[:, None, :], 0.0)
    qbd_ref[...] = qbd.reshape(nrow, MIX_A)

    run = jnp.zeros((H_A, LANES), f32)
    for g in range(ng):
        slot = g % nslot
        ahead = g + depth
        if ahead < ng:
            start_kv(b, ahead, ahead % nslot)
        else:
            @pl.when(b + 1 < db)
            def _next_seq():
                start_kv(b + 1, ahead - ng, ahead % nslot)
        wait_kv(slot)
        if hooks is not None:
            hooks[0](g)
        biases = []
        for i in range(pg):
            lf = lfbuf[lslot, g * pg + i]
            incl = _suffix_sums(lf)
            biases.append(incl - lf + run)
            run = run + incl[:, 0:1]
        bias = jnp.concatenate(biases, axis=1) * LOG2E
        s = _dot(qbd_ref[...], kbuf[slot].reshape(MIX_A, width))
        if hooks is not None:
            hooks[1](g)
        s = (s.reshape(t, H_A, width) + bias[None]).reshape(nrow, width)
        m_cur = jnp.max(s, axis=1, keepdims=True)
        if g == 0:
            m_next = jnp.broadcast_to(m_cur, (nrow, LANES))
        else:
            m_prev = m_ref[...]
            m_next = jnp.maximum(m_prev, m_cur)
        p = jnp.exp2(s - _tile_lanes(m_next, width // LANES))
        p_sum = jnp.sum(p, axis=1, keepdims=True)
        pv = _dot_nt(p, vbuf[slot].reshape(MIX_A, width))
        if g == 0:
            l_ref[...] = jnp.broadcast_to(p_sum, (nrow, LANES))
            acc_ref[...] = pv
        else:
            a = jnp.exp2(m_prev - m_next)
            l_ref[...] = a * l_ref[...] + p_sum
            acc_ref[...] = acc_ref[...] * _tile_lanes(a, MIX_A // LANES) + pv
        m_ref[...] = m_next
    if hooks is not None:
        hooks[0](ng)

    pad = jnp.zeros((t, MIX_A), f32)
    kn = jnp.concatenate([kn32, pad], axis=0)
    vn = jnp.concatenate([vn32, pad], axis=0)
    cn = jnp.concatenate([cn8, jnp.zeros((H_A, t), f32)], axis=1) * LOG2E
    sn = _dot_nt(qbd_ref[...], kn)
    sn = (sn.reshape(t, H_A, 2 * t) - cn[None]).reshape(nrow, 2 * t)
    row_t = lax.broadcasted_iota(jnp.int32, (nrow, 2 * t), 0) // H_A
    col = lax.broadcasted_iota(jnp.int32, (nrow, 2 * t), 1)
    sn = jnp.where(col <= row_t, sn, NEG)
    m_prev = m_ref[...]
    m_next = jnp.maximum(m_prev, jnp.max(sn, axis=1, keepdims=True))
    a = jnp.exp2(m_prev - m_next)
    pn = jnp.exp2(sn - m_next[:, 0:1])
    l = a * l_ref[...] + jnp.sum(pn, axis=1, keepdims=True)
    acc = acc_ref[...] * _tile_lanes(a, MIX_A // LANES) + _dot(pn, vn)
    o = acc / _tile_lanes(l, MIX_A // LANES)
    o3 = jnp.where(hsel[None], o.reshape(t, H_A, MIX_A), 0.0)
    return jnp.sum(o3, axis=1)


def _mlp_sample(x, oa, ob, wo, wu, wd, g1, g2, g3, pt_flat, ck, cv, clf, q, kn, vn, cnt, db, t, pg, tm):
    rows = x.shape[0]
    nsteps = rows // tm
    spp = db // nsteps
    n_pages = pt_flat.shape[0] // db
    ng = n_pages // pg
    assert db % nsteps == 0 and ng % KV_SLOTS == 0 and D_FF % ng == 0
    nrow = t * H_A
    row_spec = lambda width: pl.BlockSpec((tm, width), lambda i, pt: (i, 0))
    const = lambda shape: pl.BlockSpec(shape, lambda i, pt: (0, 0), pipeline_mode=pl.Buffered(1))
    seq_spec = lambda width: pl.BlockSpec((1, spp * t, width), lambda i, pt: (0, i, 0))
    hbm = pl.BlockSpec(memory_space=pl.ANY)
    grid_spec = pltpu.PrefetchScalarGridSpec(
        num_scalar_prefetch=1,
        grid=(nsteps,),
        in_specs=[row_spec(D_MODEL), row_spec(MIX_A), row_spec(MIX_B),
                  const((D_MODEL, D_MODEL)), const((D_MODEL, D_FF)), const((D_FF, D_MODEL)),
                  const((1, D_MODEL)), const((1, D_MODEL)), const((1, D_MODEL)),
                  hbm, hbm, hbm, seq_spec(MIX_A), seq_spec(MIX_A), seq_spec(MIX_A),
                  pl.BlockSpec((spp, H_A, t), lambda i, pt: (i, 0, 0))],
        out_specs=(row_spec(D_MODEL), pl.BlockSpec((spp, t, MIX_A), lambda i, pt: (i, 0, 0))),
        scratch_shapes=[
            pltpu.VMEM((KV_SLOTS, H_A, HD_A, pg * PAGE), f32),
            pltpu.VMEM((KV_SLOTS, H_A, HD_A, pg * PAGE), f32),
            pltpu.VMEM((2, n_pages, H_A, PAGE), f32),
            pltpu.SemaphoreType.DMA((KV_SLOTS,)),
            pltpu.SemaphoreType.DMA((KV_SLOTS,)),
            pltpu.SemaphoreType.DMA((2,)),
            pltpu.VMEM((nrow, MIX_A), f32),
            pltpu.VMEM((nrow, LANES), f32),
            pltpu.VMEM((nrow, LANES), f32),
            pltpu.VMEM((nrow, MIX_A), f32),
            pltpu.VMEM((tm, D_MODEL), bf16),
            pltpu.VMEM((tm, D_FF // ng), bf16),
            pltpu.VMEM((tm, D_MODEL), f32),
        ],
    )
    return pl.pallas_call(
        functools.partial(_mlp_sample_kernel, pg, n_pages, db, spp),
        grid_spec=grid_spec,
        out_shape=(jax.ShapeDtypeStruct((rows, D_MODEL), f32),
                   jax.ShapeDtypeStruct((db, t, MIX_A), bf16)),
        compiler_params=_cparams(("arbitrary",)),
        name="mlp_sample",
    )(pt_flat, x, oa, ob, wo, wu, wd, g1, g2, g3, ck, cv, clf, q, kn, vn, cnt)


def _largest_divisor(n, candidates):
    for c in candidates:
        if n % c == 0:
            return c
    raise ValueError(f"no tile size for {n}")


def _rope_tables(pos):
    half = DK_B // 2
    inv = ROPE_BASE ** (-jnp.arange(half, dtype=f32) / half)
    ang = pos[:, None] * inv[None, :]
    cos = jnp.cos(ang)
    sin = jnp.sin(ang)
    return (jnp.concatenate([cos, cos, cos, cos], axis=1),
            jnp.concatenate([-sin, sin, -sin, sin], axis=1))


def _sum_matrix(seg_id, suffix):
    i = np.arange(seg_id.shape[0])
    same = seg_id[:, None] == seg_id[None, :]
    prefix = same & (i[None, :] <= i[:, None]) & ~suffix[:, None]
    later = same & (i[None, :] > i[:, None]) & suffix[:, None]
    return jnp.asarray(prefix.astype(np.float32) - later.astype(np.float32), bf16)


def kernel(x_prompt, x_sample, cache_k, cache_v, cache_logf, state_ret, page_table, meta_tokens,
           g_pre_mix, w_in, b_f, w_out, g_post_mix, g_pre_mlp, w_up, w_down, g_post_mlp):
    nb, seq, _ = x_prompt.shape
    db, t, _ = x_sample.shape
    n_pool = cache_k.shape[1]
    n_pages = page_table.shape[1]
    n_s = db * t
    assert w_in.shape[0] == 1, "single layer"
    assert seq % 256 == 0 and n_s % LANES == 0 and cache_k.shape[2] == PAGE and n_pages % 16 == 0

    w = w_in[0]
    n_fa = 3 * MIX_A
    w_p = jnp.concatenate([w[:, :n_fa], w[:, n_fa + H_A:], w[:, n_fa:n_fa + H_A],
                           jnp.zeros((D_MODEL, D_INP - w.shape[1]), w.dtype)], axis=1).astype(bf16)
    bf_p = jnp.concatenate([b_f[0], jnp.zeros((LANES - H_A,), f32)])[None]
    g0 = g_pre_mix[0][None]
    wo = w_out[0].astype(bf16)
    wu = w_up[0].astype(bf16)
    wd = w_down[0].astype(bf16)
    g1, g2, g3 = g_post_mix[0][None], g_pre_mlp[0][None], g_post_mlp[0][None]

    rows_aux = n_s + LANES
    n_pad = LANES - N_META
    x_aux = jnp.concatenate([x_sample.reshape(n_s, D_MODEL), meta_tokens,
                             jnp.zeros((n_pad, D_MODEL), f32)], axis=0)[None]
    past = n_pages * PAGE
    pos_aux = jnp.concatenate([jnp.tile(past + jnp.arange(t, dtype=f32), db),
                               jnp.arange(N_META, dtype=f32), jnp.zeros((n_pad,), f32)])
    cos_a, sin_a = _rope_tables(pos_aux)
    seg_aux = np.concatenate([np.arange(n_s) // t, np.full((N_META,), db), np.full((n_pad,), db + 1)])
    is_meta = seg_aux == db
    (q_a, kaug_a, vt_a, k32_a, v32_a, lf_a, ccol_a, qr_a, kr_a, vr_a, gr_a) = _proj(
        False, x_aux, g0, w_p, bf_p, cos_a, sin_a, _sum_matrix(seg_aux, is_meta), rows_aux)
    meta_blk16 = n_s // N_META
    meta_blk128 = n_s // LANES

    tm = _largest_divisor(seq, (512, 256))
    cos_p, sin_p = _rope_tables(N_META + jnp.arange(seq, dtype=f32))
    one_seg = np.zeros((tm,), np.int64)
    (qaug_p, kaug_p, vt_p, k32_p, v32_p, lf_p, qr_p, kr_p, vr_p, gr_p) = _proj(
        True, x_prompt, g0, w_p, bf_p, cos_p, sin_p, _sum_matrix(one_seg, one_seg > 0), tm,
        meta=(k32_a, v32_a, lf_a, meta_blk16))

    oa_p = _fox_prompt(qaug_p, kaug_p, vt_p, kaug_a, vt_a, meta_blk128, min(tm, Q_TILE))
    ob_p, s_p = _ret_prompt(qr_p, kr_p, vr_p, gr_p, kr_a, vr_a, meta_blk16, 256)

    cnt = ccol_a[0, :n_s].reshape(db, t, H_A).transpose(0, 2, 1)
    pg = KV_GROUP
    tm_mlp = 256
    y_p, oa_s = _mlp_sample(x_prompt.reshape(nb * seq, D_MODEL), oa_p.reshape(nb * seq, MIX_A),
                            ob_p.reshape(nb * seq, MIX_B), wo, wu, wd, g1, g2, g3,
                            page_table.T.reshape(n_pages * db),
                            cache_k[0].transpose(0, 2, 3, 1), cache_v[0].transpose(0, 2, 3, 1),
                            cache_logf[0].transpose(0, 2, 1),
                            q_a, k32_a, v32_a, cnt, db, t, pg, tm_mlp)

    sb = _largest_divisor(db, (8, 4, 2))
    ob_s, s_s = _ret_sample(qr_a, kr_a, vr_a, gr_a, state_ret[0], db, t, sb)
    tms = _largest_divisor(n_s, (512, 256, 128))
    y_s = _mlp(x_sample.reshape(n_s, D_MODEL), oa_s.reshape(n_s, MIX_A), ob_s[0],
               wo, wu, wd, g1, g2, g3, tms)

    k_prompt = k32_p.reshape(1, nb, N_META + seq, H_A, HD_A)
    v_prompt = v32_p.reshape(1, nb, N_META + seq, H_A, HD_A)
    return (y_p.reshape(nb, seq, D_MODEL), y_s.reshape(db, t, D_MODEL),
            k_prompt, v_prompt, lf_p[None], s_p[None],
            k32_a[0, :n_s].reshape(1, db, t, H_A, HD_A), v32_a[0, :n_s].reshape(1, db, t, H_A, HD_A),
            lf_a[0, :n_s].reshape(1, db, t, H_A), s_s[None])
```

```python
import functools

import numpy as np
import jax
import jax.numpy as jnp
from jax import lax
from jax.experimental import pallas as pl
from jax.experimental.pallas import tpu as pltpu

f32 = jnp.float32
bf16 = jnp.bfloat16

D_MODEL = 1024
N_META = 16
PAGE = 128
HD_A = 64
H_A = 8
H_B = 4
DK_B = 64
DV_B = 128
MIX_A = H_A * HD_A
MIX_B = H_B * DV_B
D_FF = 4 * D_MODEL
ROPE_BASE = 10000.0
EPS = 1e-6
NEG = -1e30
LOG2E = 1.4426950408889634

QA, KA, VA, QR, KR, VR, GR, FA = 0, 512, 1024, 1536, 1792, 2048, 2560, 3072
D_INP = 3200
LANES = 128
SUBLANES = 8
AUG = H_A * LANES
ONE_LANE = 3 * H_A
VT_ROWS = 80
SCORES_AHEAD = 3
RET_CHUNK = 512
KV_TILE = 2048
Q_TILE = 512
KV_GROUP = 8
KV_SLOTS = 4

VMEM_LIMIT = 56 * 1024 * 1024


def _cparams(sem):
    return pltpu.CompilerParams(dimension_semantics=sem, vmem_limit_bytes=VMEM_LIMIT)


def _dot(a, b):
    return jnp.dot(a, b, preferred_element_type=f32)


def _dot_nt(a, b):
    return lax.dot_general(a, b, (((1,), (1,)), ((), ())), preferred_element_type=f32)


def _dot_tn(a, b):
    return lax.dot_general(a, b, (((0,), (0,)), ((), ())), preferred_element_type=f32)


def _split3(x):
    hi = x.astype(bf16).astype(f32)
    r = x - hi
    mid = r.astype(bf16).astype(f32)
    lo = (r - mid).astype(bf16).astype(f32)
    return hi, mid, lo


def _pack3(x, lane):
    hi, mid, lo = _split3(x)
    return (hi + pltpu.roll(mid, H_A, 1) + pltpu.roll(lo, 2 * H_A, 1)
            + jnp.where(lane == ONE_LANE, 1.0, 0.0))


def _rms(x, g):
    return x * lax.rsqrt(jnp.mean(x * x, axis=-1, keepdims=True) + EPS) * g


def _own_lanes(g, lane):
    return (lane < HD_A) if g % 2 == 0 else (lane >= HD_A)


def _placements():
    rk = np.zeros((LANES, AUG), np.float32)
    rq = np.zeros((LANES, AUG), np.float32)
    for g in range(H_A):
        base = g * LANES + (HD_A if g % 2 == 0 else 0)
        for part in range(3):
            rk[part * H_A + g, base + part] = 1.0
            rk[ONE_LANE, base + 3 + part] = 1.0
            rq[ONE_LANE, base + part] = 1.0
            rq[part * H_A + g, base + 3 + part] = 1.0
    return jnp.asarray(rk, bf16), jnp.asarray(rq, bf16)


def _proj_kernel(prompt, x_ref, g_ref, w_ref, bf_ref, cos_ref, sin_ref, a_ref, rk_ref, rq_ref, *refs):
    if prompt:
        (km_ref, vm_ref, lfm_ref, qaug_ref, kaug_ref, vt_ref, k_hbm, v_hbm, lf_hbm,
         qr_ref, kr_ref, vr_ref, gr_ref, carry_ref, kst_ref, vst_ref, lst_ref, osem, msem) = refs
    else:
        (q_ref, kaug_ref, vt_ref, k32_ref, v32_ref, lf_ref, ccol_ref,
         qr_ref, kr_ref, vr_ref, gr_ref, carry_ref) = refs
    t = pl.program_id(1)
    x = x_ref[0]
    h = _rms(x, g_ref[...]).astype(bf16)
    z = _dot(h, w_ref[...])
    tm = z.shape[0]
    lane = lax.broadcasted_iota(jnp.int32, (tm, LANES), 1)

    qs = z[:, QA:QA + MIX_A] * (HD_A ** -0.5 * LOG2E)
    k = z[:, KA:KA + MIX_A]
    v = z[:, VA:VA + MIX_A]
    if not prompt:
        k32_ref[0] = k
        v32_ref[0] = v
    vt3 = v.T.reshape(H_A, HD_A, tm)
    extra = jnp.where(lax.broadcasted_iota(jnp.int32, (H_A, VT_ROWS - HD_A, tm), 1) == 0, 1.0, 0.0)
    vt_ref[0] = jnp.concatenate([vt3, extra], axis=1).reshape(H_A * VT_ROWS, tm).astype(bf16)
    vr_ref[0] = z[:, VR:VR + MIX_B].astype(bf16)
    gr_ref[0] = z[:, GR:GR + MIX_B]

    cos = cos_ref[...]
    sin = sin_ref[...]
    first_half = (lane % DK_B) < (DK_B // 2)
    for s in range(4):
        zs = z[:, QR + s * LANES: QR + (s + 1) * LANES]
        partner = jnp.where(first_half, pltpu.roll(zs, LANES - DK_B // 2, 1),
                            pltpu.roll(zs, DK_B // 2, 1))
        r = zs * cos + partner * sin
        if s < 2:
            qr_ref[0, :, s * LANES:(s + 1) * LANES] = r
        else:
            kr_ref[0, :, (s - 2) * LANES:(s - 1) * LANES] = r * (DK_B ** -0.5)

    fa = z[:, FA:FA + LANES] + bf_ref[...]
    lf = jnp.where(lane < H_A, jnp.minimum(fa, 0.0) - jnp.log1p(jnp.exp(-jnp.abs(fa))), 0.0)
    if prompt:
        nt = pl.num_programs(1)
        step = pl.program_id(0) * nt + t
        slot = step % 2

        def out_copies(slot_, b_, t_):
            rows = pl.ds(N_META + t_ * tm, tm)
            return (pltpu.make_async_copy(kst_ref.at[slot_], k_hbm.at[b_, rows], osem.at[0, slot_]),
                    pltpu.make_async_copy(vst_ref.at[slot_], v_hbm.at[b_, rows], osem.at[1, slot_]),
                    pltpu.make_async_copy(lst_ref.at[slot_], lf_hbm.at[b_, rows], osem.at[2, slot_]))

        @pl.when(step >= 2)
        def _reuse():
            for cp in out_copies(slot, 0, 0):
                cp.wait()

        kst_ref[slot] = k
        vst_ref[slot] = v
        lst_ref[slot] = lf[:, :H_A]
        for cp in out_copies(slot, pl.program_id(0), t):
            cp.start()

        @pl.when(t == 0)
        def _meta_rows():
            head = pl.ds(0, N_META)
            cps = (pltpu.make_async_copy(km_ref.at[0], k_hbm.at[pl.program_id(0), head], msem.at[0]),
                   pltpu.make_async_copy(vm_ref.at[0], v_hbm.at[pl.program_id(0), head], msem.at[1]),
                   pltpu.make_async_copy(lfm_ref.at[0], lf_hbm.at[pl.program_id(0), head], msem.at[2]))
            for cp in cps:
                cp.start()
            for cp in cps:
                cp.wait()

        @pl.when(step == pl.num_programs(0) * nt - 1)
        def _drain():
            for cp in out_copies(slot, 0, 0) + out_copies(1 - slot, 0, 0):
                cp.wait()
    else:
        lf_ref[0] = lf[:, :H_A]
    hi, mid, lo = _split3(lf)
    parts = (hi + pltpu.roll(mid, H_A, 1) + pltpu.roll(lo, 2 * H_A, 1)).astype(bf16)
    cs = _dot(a_ref[...], parts)
    c = cs + pltpu.roll(cs, LANES - H_A, 1) + pltpu.roll(cs, LANES - 2 * H_A, 1)

    @pl.when(t == 0)
    def _():
        carry_ref[...] = jnp.zeros_like(carry_ref)

    carry = carry_ref[...]
    c = jnp.where(lane < H_A, c + carry[0:1], 0.0)
    carry_ref[...] = jnp.broadcast_to(c[tm - 1:tm], carry_ref.shape)

    e_k = _dot(_pack3(-LOG2E * c, lane).astype(bf16), rk_ref[...])
    for g in range(H_A):
        pair = k[:, (g // 2) * LANES:(g // 2 + 1) * LANES]
        kaug_ref[0, :, g * LANES:(g + 1) * LANES] = jnp.where(
            _own_lanes(g, lane), pair, e_k[:, g * LANES:(g + 1) * LANES]).astype(bf16)
    if prompt:
        lane8 = lane[:SUBLANES]
        qx = _pack3(LOG2E * carry, lane8)
        e_q = _dot(jnp.concatenate([qx, qx], axis=0).astype(bf16), rq_ref[...])
        for g in range(H_A):
            pair = qs[:, (g // 2) * LANES:(g // 2 + 1) * LANES]
            qaug_ref[0, :, g * LANES:(g + 1) * LANES] = jnp.where(
                _own_lanes(g, lane), pair, e_q[0:1, g * LANES:(g + 1) * LANES]).astype(bf16)
    else:
        q_ref[0] = qs
        ccol_ref[0] = c[:, :H_A]


def _proj(prompt, x, g, w, bfp, cos_t, sin_t, a, tm, meta=None):
    nb, rows, _ = x.shape
    nt = rows // tm
    rk, rq = _placements()
    row_spec = lambda width: pl.BlockSpec((1, tm, width), lambda b, t: (b, t, 0))
    const = lambda shape: pl.BlockSpec(shape, lambda b, t: (0,) * len(shape))
    rows_of = lambda width, dt: jax.ShapeDtypeStruct((nb, rows, width), dt)
    vt_shape = jax.ShapeDtypeStruct((nb, H_A * VT_ROWS, rows), bf16)
    vt_spec = pl.BlockSpec((1, H_A * VT_ROWS, tm), lambda b, t: (b, 0, t))
    tail_shape = (rows_of(H_B * DK_B, f32), rows_of(H_B * DK_B, f32), rows_of(MIX_B, bf16), rows_of(MIX_B, f32))
    tail_spec = (row_spec(H_B * DK_B), row_spec(H_B * DK_B), row_spec(MIX_B), row_spec(MIX_B))
    extra_in, extra_specs = (), []
    scratch = [pltpu.VMEM((SUBLANES, LANES), f32)]
    if prompt:
        assert nb * nt >= 2
        k_a, v_a, lf_a, meta_blk = meta
        full = lambda width: jax.ShapeDtypeStruct((nb, N_META + rows, width), f32)
        hbm = pl.BlockSpec(memory_space=pl.ANY)
        out_shape = (rows_of(AUG, bf16), rows_of(AUG, bf16), vt_shape, full(MIX_A), full(MIX_A),
                     full(H_A)) + tail_shape
        out_specs = (row_spec(AUG), row_spec(AUG), vt_spec, hbm, hbm, hbm) + tail_spec
        meta_spec = lambda width: pl.BlockSpec((1, N_META, width), lambda b, t: (0, meta_blk, 0))
        extra_in = (k_a, v_a, lf_a)
        extra_specs = [meta_spec(MIX_A), meta_spec(MIX_A), meta_spec(H_A)]
        scratch += [pltpu.VMEM((2, tm, MIX_A), f32), pltpu.VMEM((2, tm, MIX_A), f32),
                    pltpu.VMEM((2, tm, H_A), f32),
                    pltpu.SemaphoreType.DMA((3, 2)), pltpu.SemaphoreType.DMA((3,))]
    else:
        out_shape = (rows_of(MIX_A, f32), rows_of(AUG, bf16), vt_shape, rows_of(MIX_A, f32), rows_of(MIX_A, f32),
                     rows_of(H_A, f32), rows_of(H_A, f32)) + tail_shape
        out_specs = (row_spec(MIX_A), row_spec(AUG), vt_spec, row_spec(MIX_A), row_spec(MIX_A),
                     row_spec(H_A), row_spec(H_A)) + tail_spec
    return pl.pallas_call(
        functools.partial(_proj_kernel, prompt),
        grid=(nb, nt),
        in_specs=[
            row_spec(D_MODEL),
            const((1, D_MODEL)),
            const((D_MODEL, D_INP)),
            const((1, LANES)),
            pl.BlockSpec((tm, LANES), lambda b, t: (t, 0)),
            pl.BlockSpec((tm, LANES), lambda b, t: (t, 0)),
            const((tm, tm)),
            const((LANES, AUG)),
            const((LANES, AUG)),
        ] + extra_specs,
        out_specs=out_specs,
        out_shape=out_shape,
        scratch_shapes=scratch,
        compiler_params=_cparams(("arbitrary", "arbitrary")),
        name="proj_prompt" if prompt else "proj_aux",
    )(x, g, w, bfp, cos_t, sin_t, a, rk, rq, *extra_in)


def _fox_prompt_kernel(r, qi_tab, kj_tab, mode_tab, q_ref, k_ref, vt_ref, km_ref, vtm_ref, o_ref,
                       m_ref, acc_ref):
    p_id = pl.program_id(1)
    kj = kj_tab[p_id]
    mode = mode_tab[p_id]
    tq = q_ref.shape[1]

    def attend(kk_ref, vv_ref, keys, keep, first):
        ks = keys.stop - keys.start
        m_new, scale, pvs = [], [], []

        def scores(h):
            cols = slice(h * LANES, (h + 1) * LANES)
            return _dot_nt(kk_ref[0, keys, cols], q_ref[0, :, cols])

        ahead = [scores(h) for h in range(SCORES_AHEAD)]
        for h in range(H_A):
            st = ahead.pop(0)
            if h + SCORES_AHEAD < H_A:
                ahead.append(scores(h + SCORES_AHEAD))
            if keep is not None:
                st = jnp.where(keep, st, NEG)
            s3 = st.reshape(ks // SUBLANES, SUBLANES, tq)
            m_cur = jnp.max(jnp.max(s3, axis=0), axis=0, keepdims=True)
            if first:
                m_next = jnp.broadcast_to(m_cur, (SUBLANES, tq))
            else:
                m_prev = m_ref[h]
                m_next = jnp.maximum(m_prev, m_cur)
                a = jnp.exp2(m_prev - m_next)
                scale.append(jnp.broadcast_to(a[None], (VT_ROWS // SUBLANES, SUBLANES, tq)).reshape(VT_ROWS, tq))
            p = jnp.exp2(s3 - m_next[None]).reshape(ks, tq).astype(bf16)
            pvs.append(_dot(vv_ref[0, h * VT_ROWS:(h + 1) * VT_ROWS, keys], p))
            m_new.append(m_next)
        m_ref[...] = jnp.stack(m_new)
        pv = jnp.concatenate(pvs, axis=0)
        if first:
            acc_ref[...] = pv
        else:
            acc_ref[...] = acc_ref[...] * jnp.concatenate(scale, axis=0) + pv

    block = lambda s: slice(s * tq, (s + 1) * tq)

    @pl.when(kj == 0)
    def _meta():
        nk = km_ref.shape[1]
        attend(km_ref, vtm_ref, slice(0, nk), lax.broadcasted_iota(jnp.int32, (nk, tq), 0) < N_META, True)

    for s in range(r):
        @pl.when((mode == 0) | (mode > s + 1))
        def _before():
            attend(k_ref, vt_ref, block(s), None, False)

    causal = (lax.broadcasted_iota(jnp.int32, (tq, tq), 0) <= lax.broadcasted_iota(jnp.int32, (tq, tq), 1))
    for d in range(1, r + 1):
        @pl.when(mode == d)
        def _last():
            attend(k_ref, vt_ref, block(d - 1), causal, False)
            acc3 = acc_ref[...].reshape(H_A, VT_ROWS, tq)
            o = acc3[:, :HD_A] / acc3[:, HD_A:HD_A + 1]
            o_ref[0] = o.reshape(MIX_A, tq).T.astype(o_ref.dtype)


def _fox_prompt(q_aug, k_aug, vt, km_src, vtm_src, meta_blk, tq):
    nb, rows, _ = q_aug.shape
    nq = rows // tq
    r = max(d for d in (1, 2, 4) if nq % d == 0 and d * tq <= KV_TILE)
    tk = r * tq
    qi_np, kj_np, mode_np = [], [], []
    for i in range(nq):
        for j in range(i // r + 1):
            qi_np.append(i)
            kj_np.append(j)
            mode_np.append(i % r + 1 if j == i // r else 0)
    tabs = [jnp.asarray(np.array(a, np.int32)) for a in (qi_np, kj_np, mode_np)]
    vrows = H_A * VT_ROWS
    grid_spec = pltpu.PrefetchScalarGridSpec(
        num_scalar_prefetch=3,
        grid=(nb, len(qi_np)),
        in_specs=[
            pl.BlockSpec((1, tq, AUG), lambda b, p, qt, kt, mt: (b, qt[p], 0)),
            pl.BlockSpec((1, tk, AUG), lambda b, p, qt, kt, mt: (b, kt[p], 0)),
            pl.BlockSpec((1, vrows, tk), lambda b, p, qt, kt, mt: (b, 0, kt[p])),
            pl.BlockSpec((1, LANES, AUG), lambda b, p, qt, kt, mt: (0, meta_blk, 0)),
            pl.BlockSpec((1, vrows, LANES), lambda b, p, qt, kt, mt: (0, 0, meta_blk)),
        ],
        out_specs=pl.BlockSpec((1, tq, MIX_A), lambda b, p, qt, kt, mt: (b, qt[p], 0)),
        scratch_shapes=[
            pltpu.VMEM((H_A, SUBLANES, tq), f32),
            pltpu.VMEM((vrows, tq), f32),
        ],
    )
    return pl.pallas_call(
        functools.partial(_fox_prompt_kernel, r),
        grid_spec=grid_spec,
        out_shape=jax.ShapeDtypeStruct((nb, rows, MIX_A), bf16),
        compiler_params=_cparams(("arbitrary", "arbitrary")),
        name="fox_prompt",
    )(*tabs, q_aug, k_aug, vt, km_src, vtm_src)


def _log_gamma():
    return np.log1p(-np.exp2(-5.0 - np.arange(H_B, dtype=np.float64)))


def _ret_tables(c):
    lg = _log_gamma()
    n = np.arange(c, dtype=np.float64)
    rel = n[:, None] - n[None, :]
    dec = np.where(rel >= 0, np.exp(np.maximum(rel, 0.0)[None] * lg[:, None, None]), 0.0)
    q_dec = np.exp((n + 1.0)[None, :] * lg[:, None])
    k_dec = np.exp((c - 1.0 - n)[None, :] * lg[:, None])
    q_dec_full = np.broadcast_to(q_dec[:, :, None], (H_B, c, DV_B))
    k_dec_full = np.repeat(k_dec.T, DK_B, axis=1)
    g_c = [float(v) for v in np.exp(c * lg)]
    return (jnp.asarray(dec, f32), jnp.asarray(q_dec_full, f32), jnp.asarray(k_dec_full, f32), g_c)


def _head_norm_gate(o, gate):
    mu = jnp.mean(o, axis=-1, keepdims=True)
    d = o - mu
    var = jnp.mean(d * d, axis=-1, keepdims=True)
    y = d * lax.rsqrt(var + EPS)
    return y * (gate * jax.nn.sigmoid(gate))


def _ret_prompt_kernel(g_c, qr_ref, kr_ref, vr_ref, gr_ref, dec_ref, qdec_ref, kdec_ref,
                       krm_ref, vrm_ref, kdecm_ref, ob_ref, s_out_ref, s_ref):
    c = pl.program_id(1)

    @pl.when(c == 0)
    def _init():
        kd = (krm_ref[0] * kdecm_ref[...]).astype(bf16)
        for h in range(H_B):
            s_ref[h] = _dot_tn(kd[:, h * DK_B:(h + 1) * DK_B], vrm_ref[0, :, h * DV_B:(h + 1) * DV_B])

    q = qr_ref[0].astype(bf16)
    k = kr_ref[0]
    kb = k.astype(bf16)
    kd = (k * kdec_ref[...]).astype(bf16)
    heads = range(H_B)
    dk = lambda h: slice(h * DK_B, (h + 1) * DK_B)
    dv = lambda h: slice(h * DV_B, (h + 1) * DV_B)
    s_old = [s_ref[h] for h in heads]
    inner = [_dot_nt(q[:, dk(h)], kb[:, dk(h)]) for h in heads]
    cross = [_dot(q[:, dk(h)], s_old[h].astype(bf16)) for h in heads]
    s_ref[...] = jnp.stack([g_c[h] * s_old[h] + _dot_tn(kd[:, dk(h)], vr_ref[0, :, dv(h)]) for h in heads])
    for h in heads:
        o = _dot((inner[h] * dec_ref[h]).astype(bf16), vr_ref[0, :, dv(h)]) + cross[h] * qdec_ref[h]
        ob_ref[0, :, dv(h)] = _head_norm_gate(o, gr_ref[0, :, dv(h)]).astype(ob_ref.dtype)

    @pl.when(c == pl.num_programs(1) - 1)
    def _fin():
        s_out_ref[0] = s_ref[...]


def _ret_prompt(qr, kr, vr, gr, krm_src, vrm_src, meta_blk, chunk):
    nb, rows, _ = qr.shape
    nc = rows // chunk
    dec, qdec, kdec, g_c = _ret_tables(chunk)
    _, _, kdec_m, _ = _ret_tables(N_META)
    row_spec = lambda width: pl.BlockSpec((1, chunk, width), lambda b, c: (b, c, 0))
    const = lambda shape: pl.BlockSpec(shape, lambda b, c: (0,) * len(shape))
    return pl.pallas_call(
        functools.partial(_ret_prompt_kernel, g_c),
        grid=(nb, nc),
        in_specs=[
            row_spec(H_B * DK_B), row_spec(H_B * DK_B), row_spec(MIX_B), row_spec(MIX_B),
            const((H_B, chunk, chunk)), const((H_B, chunk, DV_B)), const((chunk, H_B * DK_B)),
            pl.BlockSpec((1, N_META, H_B * DK_B), lambda b, c: (0, meta_blk, 0)),
            pl.BlockSpec((1, N_META, MIX_B), lambda b, c: (0, meta_blk, 0)),
            const((N_META, H_B * DK_B)),
        ],
        out_specs=(row_spec(MIX_B),
                   pl.BlockSpec((1, H_B, DK_B, DV_B), lambda b, c: (b, 0, 0, 0))),
        out_shape=(jax.ShapeDtypeStruct((nb, rows, MIX_B), bf16),
                   jax.ShapeDtypeStruct((nb, H_B, DK_B, DV_B), f32)),
        scratch_shapes=[pltpu.VMEM((H_B, DK_B, DV_B), f32)],
        compiler_params=_cparams(("arbitrary", "arbitrary")),
        name="ret_prompt",
    )(qr, kr, vr, gr, dec, qdec, kdec, krm_src, vrm_src, kdec_m)


def _ret_sample_kernel(g_c, sb, qr_ref, kr_ref, vr_ref, gr_ref, st_ref, dec_ref, qdec_ref, kdec_ref,
                       ob_ref, s_out_ref):
    t = qr_ref.shape[1] // sb
    q = qr_ref[0].reshape(sb, t, H_B * DK_B)
    k = kr_ref[0].reshape(sb, t, H_B * DK_B)
    kd = k * kdec_ref[...][None]
    v = vr_ref[0].astype(f32).reshape(sb, t, MIX_B)
    gate = gr_ref[0].reshape(sb, t, MIX_B)
    for h in range(H_B):
        qh = q[:, :, h * DK_B:(h + 1) * DK_B]
        kh = k[:, :, h * DK_B:(h + 1) * DK_B]
        kdh = kd[:, :, h * DK_B:(h + 1) * DK_B]
        vh = v[:, :, h * DV_B:(h + 1) * DV_B]
        s_old = st_ref[:, h]
        inner = jnp.einsum('btd,bsd->bts', qh, kh, preferred_element_type=f32) * dec_ref[h][None]
        o = (jnp.einsum('bts,bse->bte', inner, vh, preferred_element_type=f32)
             + jnp.einsum('btd,bde->bte', qh, s_old, preferred_element_type=f32)
             * qdec_ref[h][None])
        s_out_ref[:, h] = g_c[h] * s_old + jnp.einsum('btd,bte->bde', kdh, vh,
                                                      preferred_element_type=f32)
        y = _head_norm_gate(o, gate[:, :, h * DV_B:(h + 1) * DV_B])
        ob_ref[0, :, h * DV_B:(h + 1) * DV_B] = y.reshape(sb * t, DV_B).astype(ob_ref.dtype)


def _ret_sample(qr, kr, vr, gr, state, db, t, sb):
    dec, qdec, kdec, g_c = _ret_tables(t)
    nsteps = db // sb
    row_spec = lambda width: pl.BlockSpec((1, sb * t, width), lambda i: (0, i, 0))
    const = lambda shape: pl.BlockSpec(shape, lambda i: (0,) * len(shape))
    st_spec = pl.BlockSpec((sb, H_B, DK_B, DV_B), lambda i: (i, 0, 0, 0))
    return pl.pallas_call(
        functools.partial(_ret_sample_kernel, g_c, sb),
        grid=(nsteps,),
        in_specs=[row_spec(H_B * DK_B), row_spec(H_B * DK_B), row_spec(MIX_B), row_spec(MIX_B),
                  st_spec, const((H_B, t, t)), const((H_B, t, DV_B)), const((t, H_B * DK_B))],
        out_specs=(pl.BlockSpec((1, sb * t, MIX_B), lambda i: (0, i, 0)), st_spec),
        out_shape=(jax.ShapeDtypeStruct((1, db * t, MIX_B), bf16),
                   jax.ShapeDtypeStruct((db, H_B, DK_B, DV_B), f32)),
        compiler_params=_cparams(("arbitrary",)),
        name="ret_sample",
    )(qr, kr, vr, gr, state, dec, qdec, kdec)


def _mlp_kernel(x_ref, oa_ref, ob_ref, wo_ref, wu_ref, wd_ref, g1_ref, g2_ref, g3_ref, y_ref):
    mixed = _dot(oa_ref[...], wo_ref[0:MIX_A, :]) + _dot(ob_ref[...], wo_ref[MIX_A:, :])
    x1 = x_ref[...] + _rms(mixed, g1_ref[...])
    hn = _rms(x1, g2_ref[...]).astype(bf16)
    u = jnp.square(jnp.maximum(_dot(hn, wu_ref[...]), 0.0)).astype(bf16)
    y_ref[...] = x1 + _rms(_dot(u, wd_ref[...]), g3_ref[...])


def _mlp(x, oa, ob, wo, wu, wd, g1, g2, g3, tm):
    rows = x.shape[0]
    row_spec = lambda width: pl.BlockSpec((tm, width), lambda i: (i, 0))
    const = lambda shape: pl.BlockSpec(shape, lambda i: (0, 0), pipeline_mode=pl.Buffered(1))
    return pl.pallas_call(
        _mlp_kernel,
        grid=(rows // tm,),
        in_specs=[row_spec(D_MODEL), row_spec(MIX_A), row_spec(MIX_B),
                  const((D_MODEL, D_MODEL)), const((D_MODEL, D_FF)), const((D_FF, D_MODEL)),
                  const((1, D_MODEL)), const((1, D_MODEL)), const((1, D_MODEL))],
        out_specs=row_spec(D_MODEL),
        out_shape=jax.ShapeDtypeStruct((rows, D_MODEL), f32),
        compiler_params=_cparams(("arbitrary",)),
        name="merge_mlp",
    )(x, oa, ob, wo, wu, wd, g1, g2, g3)


def _tile_lanes(x, n):
    return jnp.concatenate([x] * n, axis=1)


def _suffix_sums(x):
    lane = lax.broadcasted_iota(jnp.int32, x.shape, 1)
    s = x
    k = 1
    while k < LANES:
        s = s + jnp.where(lane + k < LANES, pltpu.roll(s, LANES - k, 1), 0.0)
        k *= 2
    return s


def _mlp_sample_kernel(pg, n_pages, db, spp, pt_ref,
                       x_ref, oa_ref, ob_ref, wo_ref, wu_ref, wd_ref, g1_ref, g2_ref, g3_ref,
                       ck_hbm, cv_hbm, clf_hbm, q_ref, kn_ref, vn_ref, cn_ref,
                       y_ref, o_ref, kbuf, vbuf, lfbuf, ksem, vsem, lfsem,
                       qbd_ref, m_ref, l_ref, acc_ref, hn_ref, u_ref, down_ref):
    ng = n_pages // pg
    t = q_ref.shape[1] // spp
    ff = D_FF // ng

    def mlp_before_scores(g):
        if g == 0:
            mixed = _dot(oa_ref[...], wo_ref[0:MIX_A, :]) + _dot(ob_ref[...], wo_ref[MIX_A:, :])
            x1 = x_ref[...] + _rms(mixed, g1_ref[...])
            y_ref[...] = x1
            hn_ref[...] = _rms(x1, g2_ref[...]).astype(bf16)
        else:
            d = _dot(u_ref[...], wd_ref[(g - 1) * ff:g * ff, :])
            if g == 1:
                down_ref[...] = d
            else:
                down_ref[...] = down_ref[...] + d

    def mlp_before_softmax(g):
        u = _dot(hn_ref[...], wu_ref[:, g * ff:(g + 1) * ff])
        u_ref[...] = jnp.square(jnp.maximum(u, 0.0)).astype(bf16)

    for s in range(spp):
        seq = pl.program_id(0) * spp + s
        o_ref[s] = _sample_attend(pg, n_pages, db, seq, pt_ref, ck_hbm, cv_hbm, clf_hbm,
                                  q_ref[0, s * t:(s + 1) * t], kn_ref[0, s * t:(s + 1) * t],
                                  vn_ref[0, s * t:(s + 1) * t], cn_ref[s],
                                  kbuf, vbuf, lfbuf, ksem, vsem, lfsem,
                                  qbd_ref, m_ref, l_ref, acc_ref,
                                  (mlp_before_scores, mlp_before_softmax) if s == 0 else None
                                  ).astype(o_ref.dtype)
    y_ref[...] = y_ref[...] + _rms(down_ref[...], g3_ref[...])


def _sample_attend(pg, n_pages, db, b, pt_ref, ck_hbm, cv_hbm, clf_hbm, q, kn32, vn32, cn8,
                   kbuf, vbuf, lfbuf, ksem, vsem, lfsem,
                   qbd_ref, m_ref, l_ref, acc_ref, hooks):
    ng = n_pages // pg
    t = q.shape[0]
    nrow = t * H_A
    width = pg * PAGE

    def page_id(seq, j):
        return pt_ref[(n_pages - 1 - j) * db + seq]

    def kv_copies(pid, slot, i):
        lanes = pl.ds(i * PAGE, PAGE)
        return (pltpu.make_async_copy(ck_hbm.at[pid], kbuf.at[slot, :, :, lanes], ksem.at[slot]),
                pltpu.make_async_copy(cv_hbm.at[pid], vbuf.at[slot, :, :, lanes], vsem.at[slot]))

    def lf_copy(pid, lslot, j):
        return pltpu.make_async_copy(clf_hbm.at[pid], lfbuf.at[lslot, j], lfsem.at[lslot])

    def start_kv(seq, g, slot):
        for i in range(pg):
            for cp in kv_copies(page_id(seq, g * pg + i), slot, i):
                cp.start()

    def wait_kv(slot):
        for i in range(pg):
            for cp in kv_copies(0, slot, i):
                cp.wait()

    def start_lf(seq, lslot):
        for j in range(n_pages):
            lf_copy(page_id(seq, j), lslot, j).start()

    lslot = b % 2
    nslot = kbuf.shape[0]
    depth = nslot - 1

    @pl.when(b == 0)
    def _prime():
        start_lf(0, 0)
        for g0 in range(depth):
            start_kv(0, g0, g0)

    for j in range(n_pages):
        lf_copy(0, lslot, j).wait()

    @pl.when(b + 1 < db)
    def _next_lf():
        start_lf(b + 1, 1 - lslot)

    hsel = (lax.broadcasted_iota(jnp.int32, (H_A, MIX_A), 1) // HD_A
            == lax.broadcasted_iota(jnp.int32, (H_A, MIX_A), 0))
    qbd = jnp.where(hsel[None], q[:, None, :], 0.0)
    qbd_ref[...] = qbd.reshape(nrow, MIX_A)

    run = jnp.zeros((H_A, LANES), f32)
    for g in range(ng):
        slot = g % nslot
        ahead = g + depth
        if ahead < ng:
            start_kv(b, ahead, ahead % nslot)
        else:
            @pl.when(b + 1 < db)
            def _next_seq():
                start_kv(b + 1, ahead - ng, ahead % nslot)
        wait_kv(slot)
        if hooks is not None:
            hooks[0](g)
        biases = []
        for i in range(pg):
            lf = lfbuf[lslot, g * pg + i]
            incl = _suffix_sums(lf)
            biases.append(incl - lf + run)
            run = run + incl[:, 0:1]
        bias = jnp.concatenate(biases, axis=1) * LOG2E
        s = _dot(qbd_ref[...], kbuf[slot].reshape(MIX_A, width))
        if hooks is not None:
            hooks[1](g)
        s = (s.reshape(t, H_A, width) + bias[None]).reshape(nrow, width)
        m_cur = jnp.max(s, axis=1, keepdims=True)
        if g == 0:
            m_next = jnp.broadcast_to(m_cur, (nrow, LANES))
        else:
            m_prev = m_ref[...]
            m_next = jnp.maximum(m_prev, m_cur)
        p = jnp.exp2(s - _tile_lanes(m_next, width // LANES))
        p_sum = jnp.sum(p, axis=1, keepdims=True)
        pv = _dot_nt(p, vbuf[slot].reshape(MIX_A, width))
        if g == 0:
            l_ref[...] = jnp.broadcast_to(p_sum, (nrow, LANES))
            acc_ref[...] = pv
        else:
            a = jnp.exp2(m_prev - m_next)
            l_ref[...] = a * l_ref[...] + p_sum
            acc_ref[...] = acc_ref[...] * _tile_lanes(a, MIX_A // LANES) + pv
        m_ref[...] = m_next
    if hooks is not None:
        hooks[0](ng)

    pad = jnp.zeros((t, MIX_A), f32)
    kn = jnp.concatenate([kn32, pad], axis=0)
    vn = jnp.concatenate([vn32, pad], axis=0)
    cn = jnp.concatenate([cn8, jnp.zeros((H_A, t), f32)], axis=1) * LOG2E
    sn = _dot_nt(qbd_ref[...], kn)
    sn = (sn.reshape(t, H_A, 2 * t) - cn[None]).reshape(nrow, 2 * t)
    row_t = lax.broadcasted_iota(jnp.int32, (nrow, 2 * t), 0) // H_A
    col = lax.broadcasted_iota(jnp.int32, (nrow, 2 * t), 1)
    sn = jnp.where(col <= row_t, sn, NEG)
    m_prev = m_ref[...]
    m_next = jnp.maximum(m_prev, jnp.max(sn, axis=1, keepdims=True))
    a = jnp.exp2(m_prev - m_next)
    pn = jnp.exp2(sn - m_next[:, 0:1])
    l = a * l_ref[...] + jnp.sum(pn, axis=1, keepdims=True)
    acc = acc_ref[...] * _tile_lanes(a, MIX_A // LANES) + _dot(pn, vn)
    o = acc / _tile_lanes(l, MIX_A // LANES)
    o3 = jnp.where(hsel[None], o.reshape(t, H_A, MIX_A), 0.0)
    return jnp.sum(o3, axis=1)


def _mlp_sample(x, oa, ob, wo, wu, wd, g1, g2, g3, pt_flat, ck, cv, clf, q, kn, vn, cnt, db, t, pg, tm):
    rows = x.shape[0]
    nsteps = rows // tm
    spp = db // nsteps
    n_pages = pt_flat.shape[0] // db
    ng = n_pages // pg
    assert db % nsteps == 0 and ng % KV_SLOTS == 0 and D_FF % ng == 0
    nrow = t * H_A
    row_spec = lambda width: pl.BlockSpec((tm, width), lambda i, pt: (i, 0))
    const = lambda shape: pl.BlockSpec(shape, lambda i, pt: (0, 0), pipeline_mode=pl.Buffered(1))
    seq_spec = lambda width: pl.BlockSpec((1, spp * t, width), lambda i, pt: (0, i, 0))
    hbm = pl.BlockSpec(memory_space=pl.ANY)
    grid_spec = pltpu.PrefetchScalarGridSpec(
        num_scalar_prefetch=1,
        grid=(nsteps,),
        in_specs=[row_spec(D_MODEL), row_spec(MIX_A), row_spec(MIX_B),
                  const((D_MODEL, D_MODEL)), const((D_MODEL, D_FF)), const((D_FF, D_MODEL)),
                  const((1, D_MODEL)), const((1, D_MODEL)), const((1, D_MODEL)),
                  hbm, hbm, hbm, seq_spec(MIX_A), seq_spec(MIX_A), seq_spec(MIX_A),
                  pl.BlockSpec((spp, H_A, t), lambda i, pt: (i, 0, 0))],
        out_specs=(row_spec(D_MODEL), pl.BlockSpec((spp, t, MIX_A), lambda i, pt: (i, 0, 0))),
        scratch_shapes=[
            pltpu.VMEM((KV_SLOTS, H_A, HD_A, pg * PAGE), f32),
            pltpu.VMEM((KV_SLOTS, H_A, HD_A, pg * PAGE), f32),
            pltpu.VMEM((2, n_pages, H_A, PAGE), f32),
            pltpu.SemaphoreType.DMA((KV_SLOTS,)),
            pltpu.SemaphoreType.DMA((KV_SLOTS,)),
            pltpu.SemaphoreType.DMA((2,)),
            pltpu.VMEM((nrow, MIX_A), f32),
            pltpu.VMEM((nrow, LANES), f32),
            pltpu.VMEM((nrow, LANES), f32),
            pltpu.VMEM((nrow, MIX_A), f32),
            pltpu.VMEM((tm, D_MODEL), bf16),
            pltpu.VMEM((tm, D_FF // ng), bf16),
            pltpu.VMEM((tm, D_MODEL), f32),
        ],
    )
    return pl.pallas_call(
        functools.partial(_mlp_sample_kernel, pg, n_pages, db, spp),
        grid_spec=grid_spec,
        out_shape=(jax.ShapeDtypeStruct((rows, D_MODEL), f32),
                   jax.ShapeDtypeStruct((db, t, MIX_A), bf16)),
        compiler_params=_cparams(("arbitrary",)),
        name="mlp_sample",
    )(pt_flat, x, oa, ob, wo, wu, wd, g1, g2, g3, ck, cv, clf, q, kn, vn, cnt)


def _largest_divisor(n, candidates):
    for c in candidates:
        if n % c == 0:
            return c
    raise ValueError(f"no tile size for {n}")


def _rope_tables(pos):
    half = DK_B // 2
    inv = ROPE_BASE ** (-jnp.arange(half, dtype=f32) / half)
    ang = pos[:, None] * inv[None, :]
    cos = jnp.cos(ang)
    sin = jnp.sin(ang)
    return (jnp.concatenate([cos, cos, cos, cos], axis=1),
            jnp.concatenate([-sin, sin, -sin, sin], axis=1))


def _sum_matrix(seg_id, suffix):
    i = np.arange(seg_id.shape[0])
    same = seg_id[:, None] == seg_id[None, :]
    prefix = same & (i[None, :] <= i[:, None]) & ~suffix[:, None]
    later = same & (i[None, :] > i[:, None]) & suffix[:, None]
    return jnp.asarray(prefix.astype(np.float32) - later.astype(np.float32), bf16)


def kernel(x_prompt, x_sample, cache_k, cache_v, cache_logf, state_ret, page_table, meta_tokens,
           g_pre_mix, w_in, b_f, w_out, g_post_mix, g_pre_mlp, w_up, w_down, g_post_mlp):
    nb, seq, _ = x_prompt.shape
    db, t, _ = x_sample.shape
    n_pool = cache_k.shape[1]
    n_pages = page_table.shape[1]
    n_s = db * t
    assert w_in.shape[0] == 1, "single layer"
    assert seq % 256 == 0 and n_s % LANES == 0 and cache_k.shape[2] == PAGE and n_pages % 16 == 0

    w = w_in[0]
    n_fa = 3 * MIX_A
    w_p = jnp.concatenate([w[:, :n_fa], w[:, n_fa + H_A:], w[:, n_fa:n_fa + H_A],
                           jnp.zeros((D_MODEL, D_INP - w.shape[1]), w.dtype)], axis=1).astype(bf16)
    bf_p = jnp.concatenate([b_f[0], jnp.zeros((LANES - H_A,), f32)])[None]
    g0 = g_pre_mix[0][None]
    wo = w_out[0].astype(bf16)
    wu = w_up[0].astype(bf16)
    wd = w_down[0].astype(bf16)
    g1, g2, g3 = g_post_mix[0][None], g_pre_mlp[0][None], g_post_mlp[0][None]

    rows_aux = n_s + LANES
    n_pad = LANES - N_META
    x_aux = jnp.concatenate([x_sample.reshape(n_s, D_MODEL), meta_tokens,
                             jnp.zeros((n_pad, D_MODEL), f32)], axis=0)[None]
    past = n_pages * PAGE
    pos_aux = jnp.concatenate([jnp.tile(past + jnp.arange(t, dtype=f32), db),
                               jnp.arange(N_META, dtype=f32), jnp.zeros((n_pad,), f32)])
    cos_a, sin_a = _rope_tables(pos_aux)
    seg_aux = np.concatenate([np.arange(n_s) // t, np.full((N_META,), db), np.full((n_pad,), db + 1)])
    is_meta = seg_aux == db
    (q_a, kaug_a, vt_a, k32_a, v32_a, lf_a, ccol_a, qr_a, kr_a, vr_a, gr_a) = _proj(
        False, x_aux, g0, w_p, bf_p, cos_a, sin_a, _sum_matrix(seg_aux, is_meta), rows_aux)
    meta_blk16 = n_s // N_META
    meta_blk128 = n_s // LANES

    tm = _largest_divisor(seq, (512, 256))
    cos_p, sin_p = _rope_tables(N_META + jnp.arange(seq, dtype=f32))
    one_seg = np.zeros((tm,), np.int64)
    (qaug_p, kaug_p, vt_p, k32_p, v32_p, lf_p, qr_p, kr_p, vr_p, gr_p) = _proj(
        True, x_prompt, g0, w_p, bf_p, cos_p, sin_p, _sum_matrix(one_seg, one_seg > 0), tm,
        meta=(k32_a, v32_a, lf_a, meta_blk16))

    oa_p = _fox_prompt(qaug_p, kaug_p, vt_p, kaug_a, vt_a, meta_blk128, min(tm, Q_TILE))
    ob_p, s_p = _ret_prompt(qr_p, kr_p, vr_p, gr_p, kr_a, vr_a, meta_blk16, RET_CHUNK)

    cnt = ccol_a[0, :n_s].reshape(db, t, H_A).transpose(0, 2, 1)
    pg = KV_GROUP
    tm_mlp = 256
    y_p, oa_s = _mlp_sample(x_prompt.reshape(nb * seq, D_MODEL), oa_p.reshape(nb * seq, MIX_A),
                            ob_p.reshape(nb * seq, MIX_B), wo, wu, wd, g1, g2, g3,
                            page_table.T.reshape(n_pages * db),
                            cache_k[0].transpose(0, 2, 3, 1), cache_v[0].transpose(0, 2, 3, 1),
                            cache_logf[0].transpose(0, 2, 1),
                            q_a, k32_a, v32_a, cnt, db, t, pg, tm_mlp)

    sb = _largest_divisor(db, (8, 4, 2))
    ob_s, s_s = _ret_sample(qr_a, kr_a, vr_a, gr_a, state_ret[0], db, t, sb)
    tms = _largest_divisor(n_s, (512, 256, 128))
    y_s = _mlp(x_sample.reshape(n_s, D_MODEL), oa_s.reshape(n_s, MIX_A), ob_s[0],
               wo, wu, wd, g1, g2, g3, tms)

    k_prompt = k32_p.reshape(1, nb, N_META + seq, H_A, HD_A)
    v_prompt = v32_p.reshape(1, nb, N_META + seq, H_A, HD_A)
    return (y_p.reshape(nb, seq, D_MODEL), y_s.reshape(db, t, D_MODEL),
            k_prompt, v_prompt, lf_p[None], s_p[None],
            k32_a[0, :n_s].reshape(1, db, t, H_A, HD_A), v32_a[0, :n_s].reshape(1, db, t, H_A, HD_A),
            lf_a[0, :n_s].reshape(1, db, t, H_A), s_s[None])
```

```python
import functools

import numpy as np
import jax
import jax.numpy as jnp
from jax import lax
from jax.experimental import pallas as pl
from jax.experimental.pallas import tpu as pltpu

f32 = jnp.float32
bf16 = jnp.bfloat16

D_MODEL = 1024
N_META = 16
PAGE = 128
HD_A = 64
H_A = 8
H_B = 4
DK_B = 64
DV_B = 128
MIX_A = H_A * HD_A
MIX_B = H_B * DV_B
D_FF = 4 * D_MODEL
ROPE_BASE = 10000.0
EPS = 1e-6
NEG = -1e30
LOG2E = 1.4426950408889634

QA, KA, VA, QR, KR, VR, GR, FA = 0, 512, 1024, 1536, 1792, 2048, 2560, 3072
D_INP = 3200
LANES = 128
SUBLANES = 8
AUG = H_A * LANES
ONE_LANE = 3 * H_A
VT_ROWS = 80
SCORES_AHEAD = 3
RET_CHUNK = 512
KV_TILE = 2048
Q_TILE = 512
KV_GROUP = 8
KV_SLOTS = 4

VMEM_LIMIT = 56 * 1024 * 1024


def _cparams(sem):
    return pltpu.CompilerParams(dimension_semantics=sem, vmem_limit_bytes=VMEM_LIMIT)


def _dot(a, b):
    return jnp.dot(a, b, preferred_element_type=f32)


def _dot_nt(a, b):
    return lax.dot_general(a, b, (((1,), (1,)), ((), ())), preferred_element_type=f32)


def _dot_tn(a, b):
    return lax.dot_general(a, b, (((0,), (0,)), ((), ())), preferred_element_type=f32)


def _split3(x):
    hi = x.astype(bf16).astype(f32)
    r = x - hi
    mid = r.astype(bf16).astype(f32)
    lo = (r - mid).astype(bf16).astype(f32)
    return hi, mid, lo


def _pack3(x, lane):
    hi, mid, lo = _split3(x)
    return (hi + pltpu.roll(mid, H_A, 1) + pltpu.roll(lo, 2 * H_A, 1)
            + jnp.where(lane == ONE_LANE, 1.0, 0.0))


def _rms(x, g):
    return x * lax.rsqrt(jnp.mean(x * x, axis=-1, keepdims=True) + EPS) * g


def _own_lanes(g, lane):
    return (lane < HD_A) if g % 2 == 0 else (lane >= HD_A)


def _placements():
    rk = np.zeros((LANES, AUG), np.float32)
    rq = np.zeros((LANES, AUG), np.float32)
    for g in range(H_A):
        base = g * LANES + (HD_A if g % 2 == 0 else 0)
        for part in range(3):
            rk[part * H_A + g, base + part] = 1.0
            rk[ONE_LANE, base + 3 + part] = 1.0
            rq[ONE_LANE, base + part] = 1.0
            rq[part * H_A + g, base + 3 + part] = 1.0
    return jnp.asarray(rk, bf16), jnp.asarray(rq, bf16)


def _proj_kernel(prompt, x_ref, g_ref, w_ref, bf_ref, cos_ref, sin_ref, a_ref, rk_ref, rq_ref, *refs):
    if prompt:
        (km_ref, vm_ref, lfm_ref, qaug_ref, kaug_ref, vt_ref, k_hbm, v_hbm, lf_hbm,
         qr_ref, kr_ref, vr_ref, gr_ref, carry_ref, kst_ref, vst_ref, lst_ref, osem, msem) = refs
    else:
        (q_ref, kaug_ref, vt_ref, k32_ref, v32_ref, lf_ref, ccol_ref,
         qr_ref, kr_ref, vr_ref, gr_ref, carry_ref) = refs
    t = pl.program_id(1)
    x = x_ref[0]
    h = _rms(x, g_ref[...]).astype(bf16)
    z = _dot(h, w_ref[...])
    tm = z.shape[0]
    lane = lax.broadcasted_iota(jnp.int32, (tm, LANES), 1)

    qs = z[:, QA:QA + MIX_A] * (HD_A ** -0.5 * LOG2E)
    k = z[:, KA:KA + MIX_A]
    v = z[:, VA:VA + MIX_A]
    if not prompt:
        k32_ref[0] = k
        v32_ref[0] = v
    vt3 = v.T.reshape(H_A, HD_A, tm)
    extra = jnp.where(lax.broadcasted_iota(jnp.int32, (H_A, VT_ROWS - HD_A, tm), 1) == 0, 1.0, 0.0)
    vt_ref[0] = jnp.concatenate([vt3, extra], axis=1).reshape(H_A * VT_ROWS, tm).astype(bf16)
    vr_ref[0] = z[:, VR:VR + MIX_B].astype(bf16)
    gr_ref[0] = z[:, GR:GR + MIX_B]

    cos = cos_ref[...]
    sin = sin_ref[...]
    first_half = (lane % DK_B) < (DK_B // 2)
    for s in range(4):
        zs = z[:, QR + s * LANES: QR + (s + 1) * LANES]
        partner = jnp.where(first_half, pltpu.roll(zs, LANES - DK_B // 2, 1),
                            pltpu.roll(zs, DK_B // 2, 1))
        r = zs * cos + partner * sin
        if s < 2:
            qr_ref[0, :, s * LANES:(s + 1) * LANES] = r
        else:
            kr_ref[0, :, (s - 2) * LANES:(s - 1) * LANES] = r * (DK_B ** -0.5)

    fa = z[:, FA:FA + LANES] + bf_ref[...]
    lf = jnp.where(lane < H_A, jnp.minimum(fa, 0.0) - jnp.log1p(jnp.exp(-jnp.abs(fa))), 0.0)
    if prompt:
        nt = pl.num_programs(1)
        step = pl.program_id(0) * nt + t
        slot = step % 2

        def out_copies(slot_, b_, t_):
            rows = pl.ds(N_META + t_ * tm, tm)
            return (pltpu.make_async_copy(kst_ref.at[slot_], k_hbm.at[b_, rows], osem.at[0, slot_]),
                    pltpu.make_async_copy(vst_ref.at[slot_], v_hbm.at[b_, rows], osem.at[1, slot_]),
                    pltpu.make_async_copy(lst_ref.at[slot_], lf_hbm.at[b_, rows], osem.at[2, slot_]))

        @pl.when(step >= 2)
        def _reuse():
            for cp in out_copies(slot, 0, 0):
                cp.wait()

        kst_ref[slot] = k
        vst_ref[slot] = v
        lst_ref[slot] = lf[:, :H_A]
        for cp in out_copies(slot, pl.program_id(0), t):
            cp.start()

        @pl.when(t == 0)
        def _meta_rows():
            head = pl.ds(0, N_META)
            cps = (pltpu.make_async_copy(km_ref.at[0], k_hbm.at[pl.program_id(0), head], msem.at[0]),
                   pltpu.make_async_copy(vm_ref.at[0], v_hbm.at[pl.program_id(0), head], msem.at[1]),
                   pltpu.make_async_copy(lfm_ref.at[0], lf_hbm.at[pl.program_id(0), head], msem.at[2]))
            for cp in cps:
                cp.start()
            for cp in cps:
                cp.wait()

        @pl.when(step == pl.num_programs(0) * nt - 1)
        def _drain():
            for cp in out_copies(slot, 0, 0) + out_copies(1 - slot, 0, 0):
                cp.wait()
    else:
        lf_ref[0] = lf[:, :H_A]
    hi, mid, lo = _split3(lf)
    parts = (hi + pltpu.roll(mid, H_A, 1) + pltpu.roll(lo, 2 * H_A, 1)).astype(bf16)
    cs = _dot(a_ref[...], parts)
    c = cs + pltpu.roll(cs, LANES - H_A, 1) + pltpu.roll(cs, LANES - 2 * H_A, 1)

    @pl.when(t == 0)
    def _():
        carry_ref[...] = jnp.zeros_like(carry_ref)

    carry = carry_ref[...]
    c = jnp.where(lane < H_A, c + carry[0:1], 0.0)
    carry_ref[...] = jnp.broadcast_to(c[tm - 1:tm], carry_ref.shape)

    e_k = _dot(_pack3(-LOG2E * c, lane).astype(bf16), rk_ref[...])
    for g in range(H_A):
        pair = k[:, (g // 2) * LANES:(g // 2 + 1) * LANES]
        kaug_ref[0, :, g * LANES:(g + 1) * LANES] = jnp.where(
            _own_lanes(g, lane), pair, e_k[:, g * LANES:(g + 1) * LANES]).astype(bf16)
    if prompt:
        lane8 = lane[:SUBLANES]
        qx = _pack3(LOG2E * carry, lane8)
        e_q = _dot(jnp.concatenate([qx, qx], axis=0).astype(bf16), rq_ref[...])
        for g in range(H_A):
            pair = qs[:, (g // 2) * LANES:(g // 2 + 1) * LANES]
            qaug_ref[0, :, g * LANES:(g + 1) * LANES] = jnp.where(
                _own_lanes(g, lane), pair, e_q[0:1, g * LANES:(g + 1) * LANES]).astype(bf16)
    else:
        q_ref[0] = qs
        ccol_ref[0] = c[:, :H_A]


def _proj(prompt, x, g, w, bfp, cos_t, sin_t, a, tm, meta=None):
    nb, rows, _ = x.shape
    nt = rows // tm
    rk, rq = _placements()
    row_spec = lambda width: pl.BlockSpec((1, tm, width), lambda b, t: (b, t, 0))
    const = lambda shape: pl.BlockSpec(shape, lambda b, t: (0,) * len(shape))
    rows_of = lambda width, dt: jax.ShapeDtypeStruct((nb, rows, width), dt)
    vt_shape = jax.ShapeDtypeStruct((nb, H_A * VT_ROWS, rows), bf16)
    vt_spec = pl.BlockSpec((1, H_A * VT_ROWS, tm), lambda b, t: (b, 0, t))
    tail_shape = (rows_of(H_B * DK_B, f32), rows_of(H_B * DK_B, f32), rows_of(MIX_B, bf16), rows_of(MIX_B, f32))
    tail_spec = (row_spec(H_B * DK_B), row_spec(H_B * DK_B), row_spec(MIX_B), row_spec(MIX_B))
    extra_in, extra_specs = (), []
    scratch = [pltpu.VMEM((SUBLANES, LANES), f32)]
    if prompt:
        assert nb * nt >= 2
        k_a, v_a, lf_a, meta_blk = meta
        full = lambda width: jax.ShapeDtypeStruct((nb, N_META + rows, width), f32)
        hbm = pl.BlockSpec(memory_space=pl.ANY)
        out_shape = (rows_of(AUG, bf16), rows_of(AUG, bf16), vt_shape, full(MIX_A), full(MIX_A),
                     full(H_A)) + tail_shape
        out_specs = (row_spec(AUG), row_spec(AUG), vt_spec, hbm, hbm, hbm) + tail_spec
        meta_spec = lambda width: pl.BlockSpec((1, N_META, width), lambda b, t: (0, meta_blk, 0))
        extra_in = (k_a, v_a, lf_a)
        extra_specs = [meta_spec(MIX_A), meta_spec(MIX_A), meta_spec(H_A)]
        scratch += [pltpu.VMEM((2, tm, MIX_A), f32), pltpu.VMEM((2, tm, MIX_A), f32),
                    pltpu.VMEM((2, tm, H_A), f32),
                    pltpu.SemaphoreType.DMA((3, 2)), pltpu.SemaphoreType.DMA((3,))]
    else:
        out_shape = (rows_of(MIX_A, f32), rows_of(AUG, bf16), vt_shape, rows_of(MIX_A, f32), rows_of(MIX_A, f32),
                     rows_of(H_A, f32), rows_of(H_A, f32)) + tail_shape
        out_specs = (row_spec(MIX_A), row_spec(AUG), vt_spec, row_spec(MIX_A), row_spec(MIX_A),
                     row_spec(H_A), row_spec(H_A)) + tail_spec
    return pl.pallas_call(
        functools.partial(_proj_kernel, prompt),
        grid=(nb, nt),
        in_specs=[
            row_spec(D_MODEL),
            const((1, D_MODEL)),
            const((D_MODEL, D_INP)),
            const((1, LANES)),
            pl.BlockSpec((tm, LANES), lambda b, t: (t, 0)),
            pl.BlockSpec((tm, LANES), lambda b, t: (t, 0)),
            const((tm, tm)),
            const((LANES, AUG)),
            const((LANES, AUG)),
        ] + extra_specs,
        out_specs=out_specs,
        out_shape=out_shape,
        scratch_shapes=scratch,
        compiler_params=_cparams(("arbitrary", "arbitrary")),
        name="proj_prompt" if prompt else "proj_aux",
    )(x, g, w, bfp, cos_t, sin_t, a, rk, rq, *extra_in)


def _fox_prompt_kernel(r, qi_tab, kj_tab, mode_tab, q_ref, k_ref, vt_ref, km_ref, vtm_ref, o_ref,
                       m_ref, acc_ref, s_ref):
    p_id = pl.program_id(1)
    kj = kj_tab[p_id]
    mode = mode_tab[p_id]
    tq = q_ref.shape[1]

    def attend(kk_ref, vv_ref, keys, keep, first):
        ks = keys.stop - keys.start
        m_new, scale, pvs = [], [], []

        nbuf = s_ref.shape[0]

        def scores(h):
            cols = slice(h * LANES, (h + 1) * LANES)
            st = _dot_nt(kk_ref[0, keys, cols], q_ref[0, :, cols])
            if keep is not None:
                st = jnp.where(keep, st, NEG)
            s_ref[h % nbuf, 0:ks, :] = st

        for h in range(SCORES_AHEAD):
            scores(h)
        for h in range(H_A):
            if h + SCORES_AHEAD < H_A:
                scores(h + SCORES_AHEAD)
            s3 = s_ref[h % nbuf, 0:ks, :].reshape(ks // SUBLANES, SUBLANES, tq)
            m_cur = jnp.max(jnp.max(s3, axis=0), axis=0, keepdims=True)
            s3 = s_ref[h % nbuf, 0:ks, :].reshape(ks // SUBLANES, SUBLANES, tq)
            if first:
                m_next = jnp.broadcast_to(m_cur, (SUBLANES, tq))
            else:
                m_prev = m_ref[h]
                m_next = jnp.maximum(m_prev, m_cur)
                a = jnp.exp2(m_prev - m_next)
                scale.append(jnp.broadcast_to(a[None], (VT_ROWS // SUBLANES, SUBLANES, tq)).reshape(VT_ROWS, tq))
            p = jnp.exp2(s3 - m_next[None]).reshape(ks, tq).astype(bf16)
            pvs.append(_dot(vv_ref[0, h * VT_ROWS:(h + 1) * VT_ROWS, keys], p))
            m_new.append(m_next)
        m_ref[...] = jnp.stack(m_new)
        pv = jnp.concatenate(pvs, axis=0)
        if first:
            acc_ref[...] = pv
        else:
            acc_ref[...] = acc_ref[...] * jnp.concatenate(scale, axis=0) + pv

    block = lambda s: slice(s * tq, (s + 1) * tq)

    @pl.when(kj == 0)
    def _meta():
        nk = km_ref.shape[1]
        attend(km_ref, vtm_ref, slice(0, nk), lax.broadcasted_iota(jnp.int32, (nk, tq), 0) < N_META, True)

    for s in range(r):
        @pl.when((mode == 0) | (mode > s + 1))
        def _before():
            attend(k_ref, vt_ref, block(s), None, False)

    causal = (lax.broadcasted_iota(jnp.int32, (tq, tq), 0) <= lax.broadcasted_iota(jnp.int32, (tq, tq), 1))
    for d in range(1, r + 1):
        @pl.when(mode == d)
        def _last():
            attend(k_ref, vt_ref, block(d - 1), causal, False)
            acc3 = acc_ref[...].reshape(H_A, VT_ROWS, tq)
            o = acc3[:, :HD_A] / acc3[:, HD_A:HD_A + 1]
            o_ref[0] = o.reshape(MIX_A, tq).T.astype(o_ref.dtype)


def _fox_prompt(q_aug, k_aug, vt, km_src, vtm_src, meta_blk, tq):
    nb, rows, _ = q_aug.shape
    nq = rows // tq
    r = max(d for d in (1, 2, 4) if nq % d == 0 and d * tq <= KV_TILE)
    tk = r * tq
    qi_np, kj_np, mode_np = [], [], []
    for i in range(nq):
        for j in range(i // r + 1):
            qi_np.append(i)
            kj_np.append(j)
            mode_np.append(i % r + 1 if j == i // r else 0)
    tabs = [jnp.asarray(np.array(a, np.int32)) for a in (qi_np, kj_np, mode_np)]
    vrows = H_A * VT_ROWS
    grid_spec = pltpu.PrefetchScalarGridSpec(
        num_scalar_prefetch=3,
        grid=(nb, len(qi_np)),
        in_specs=[
            pl.BlockSpec((1, tq, AUG), lambda b, p, qt, kt, mt: (b, qt[p], 0)),
            pl.BlockSpec((1, tk, AUG), lambda b, p, qt, kt, mt: (b, kt[p], 0)),
            pl.BlockSpec((1, vrows, tk), lambda b, p, qt, kt, mt: (b, 0, kt[p])),
            pl.BlockSpec((1, LANES, AUG), lambda b, p, qt, kt, mt: (0, meta_blk, 0)),
            pl.BlockSpec((1, vrows, LANES), lambda b, p, qt, kt, mt: (0, 0, meta_blk)),
        ],
        out_specs=pl.BlockSpec((1, tq, MIX_A), lambda b, p, qt, kt, mt: (b, qt[p], 0)),
        scratch_shapes=[
            pltpu.VMEM((H_A, SUBLANES, tq), f32),
            pltpu.VMEM((vrows, tq), f32),
            pltpu.VMEM((SCORES_AHEAD + 1, tq, tq), f32),
        ],
    )
    return pl.pallas_call(
        functools.partial(_fox_prompt_kernel, r),
        grid_spec=grid_spec,
        out_shape=jax.ShapeDtypeStruct((nb, rows, MIX_A), bf16),
        compiler_params=_cparams(("arbitrary", "arbitrary")),
        name="fox_prompt",
    )(*tabs, q_aug, k_aug, vt, km_src, vtm_src)


def _log_gamma():
    return np.log1p(-np.exp2(-5.0 - np.arange(H_B, dtype=np.float64)))


def _ret_tables(c):
    lg = _log_gamma()
    n = np.arange(c, dtype=np.float64)
    rel = n[:, None] - n[None, :]
    dec = np.where(rel >= 0, np.exp(np.maximum(rel, 0.0)[None] * lg[:, None, None]), 0.0)
    q_dec = np.exp((n + 1.0)[None, :] * lg[:, None])
    k_dec = np.exp((c - 1.0 - n)[None, :] * lg[:, None])
    q_dec_full = np.broadcast_to(q_dec[:, :, None], (H_B, c, DV_B))
    k_dec_full = np.repeat(k_dec.T, DK_B, axis=1)
    g_c = [float(v) for v in np.exp(c * lg)]
    return (jnp.asarray(dec, f32), jnp.asarray(q_dec_full, f32), jnp.asarray(k_dec_full, f32), g_c)


def _head_norm_gate(o, gate):
    mu = jnp.mean(o, axis=-1, keepdims=True)
    d = o - mu
    var = jnp.mean(d * d, axis=-1, keepdims=True)
    y = d * lax.rsqrt(var + EPS)
    return y * (gate * jax.nn.sigmoid(gate))


def _ret_prompt_kernel(g_c, qr_ref, kr_ref, vr_ref, gr_ref, dec_ref, qdec_ref, kdec_ref,
                       krm_ref, vrm_ref, kdecm_ref, ob_ref, s_out_ref, s_ref):
    c = pl.program_id(1)

    @pl.when(c == 0)
    def _init():
        kd = (krm_ref[0] * kdecm_ref[...]).astype(bf16)
        for h in range(H_B):
            s_ref[h] = _dot_tn(kd[:, h * DK_B:(h + 1) * DK_B], vrm_ref[0, :, h * DV_B:(h + 1) * DV_B])

    q = qr_ref[0].astype(bf16)
    k = kr_ref[0]
    kb = k.astype(bf16)
    kd = (k * kdec_ref[...]).astype(bf16)
    heads = range(H_B)
    dk = lambda h: slice(h * DK_B, (h + 1) * DK_B)
    dv = lambda h: slice(h * DV_B, (h + 1) * DV_B)
    s_old = [s_ref[h] for h in heads]
    inner = [_dot_nt(q[:, dk(h)], kb[:, dk(h)]) for h in heads]
    cross = [_dot(q[:, dk(h)], s_old[h].astype(bf16)) for h in heads]
    s_ref[...] = jnp.stack([g_c[h] * s_old[h] + _dot_tn(kd[:, dk(h)], vr_ref[0, :, dv(h)]) for h in heads])
    for h in heads:
        o = _dot((inner[h] * dec_ref[h]).astype(bf16), vr_ref[0, :, dv(h)]) + cross[h] * qdec_ref[h]
        ob_ref[0, :, dv(h)] = _head_norm_gate(o, gr_ref[0, :, dv(h)]).astype(ob_ref.dtype)

    @pl.when(c == pl.num_programs(1) - 1)
    def _fin():
        s_out_ref[0] = s_ref[...]


def _ret_prompt(qr, kr, vr, gr, krm_src, vrm_src, meta_blk, chunk):
    nb, rows, _ = qr.shape
    nc = rows // chunk
    dec, qdec, kdec, g_c = _ret_tables(chunk)
    _, _, kdec_m, _ = _ret_tables(N_META)
    row_spec = lambda width: pl.BlockSpec((1, chunk, width), lambda b, c: (b, c, 0))
    const = lambda shape: pl.BlockSpec(shape, lambda b, c: (0,) * len(shape))
    return pl.pallas_call(
        functools.partial(_ret_prompt_kernel, g_c),
        grid=(nb, nc),
        in_specs=[
            row_spec(H_B * DK_B), row_spec(H_B * DK_B), row_spec(MIX_B), row_spec(MIX_B),
            const((H_B, chunk, chunk)), const((H_B, chunk, DV_B)), const((chunk, H_B * DK_B)),
            pl.BlockSpec((1, N_META, H_B * DK_B), lambda b, c: (0, meta_blk, 0)),
            pl.BlockSpec((1, N_META, MIX_B), lambda b, c: (0, meta_blk, 0)),
            const((N_META, H_B * DK_B)),
        ],
        out_specs=(row_spec(MIX_B),
                   pl.BlockSpec((1, H_B, DK_B, DV_B), lambda b, c: (b, 0, 0, 0))),
        out_shape=(jax.ShapeDtypeStruct((nb, rows, MIX_B), bf16),
                   jax.ShapeDtypeStruct((nb, H_B, DK_B, DV_B), f32)),
        scratch_shapes=[pltpu.VMEM((H_B, DK_B, DV_B), f32)],
        compiler_params=_cparams(("arbitrary", "arbitrary")),
        name="ret_prompt",
    )(qr, kr, vr, gr, dec, qdec, kdec, krm_src, vrm_src, kdec_m)


def _ret_sample_kernel(g_c, sb, qr_ref, kr_ref, vr_ref, gr_ref, st_ref, dec_ref, qdec_ref, kdec_ref,
                       ob_ref, s_out_ref):
    t = qr_ref.shape[1] // sb
    q = qr_ref[0].reshape(sb, t, H_B * DK_B)
    k = kr_ref[0].reshape(sb, t, H_B * DK_B)
    kd = k * kdec_ref[...][None]
    v = vr_ref[0].astype(f32).reshape(sb, t, MIX_B)
    gate = gr_ref[0].reshape(sb, t, MIX_B)
    for h in range(H_B):
        qh = q[:, :, h * DK_B:(h + 1) * DK_B]
        kh = k[:, :, h * DK_B:(h + 1) * DK_B]
        kdh = kd[:, :, h * DK_B:(h + 1) * DK_B]
        vh = v[:, :, h * DV_B:(h + 1) * DV_B]
        s_old = st_ref[:, h]
        inner = jnp.einsum('btd,bsd->bts', qh, kh, preferred_element_type=f32) * dec_ref[h][None]
        o = (jnp.einsum('bts,bse->bte', inner, vh, preferred_element_type=f32)
             + jnp.einsum('btd,bde->bte', qh, s_old, preferred_element_type=f32)
             * qdec_ref[h][None])
        s_out_ref[:, h] = g_c[h] * s_old + jnp.einsum('btd,bte->bde', kdh, vh,
                                                      preferred_element_type=f32)
        y = _head_norm_gate(o, gate[:, :, h * DV_B:(h + 1) * DV_B])
        ob_ref[0, :, h * DV_B:(h + 1) * DV_B] = y.reshape(sb * t, DV_B).astype(ob_ref.dtype)


def _ret_sample(qr, kr, vr, gr, state, db, t, sb):
    dec, qdec, kdec, g_c = _ret_tables(t)
    nsteps = db // sb
    row_spec = lambda width: pl.BlockSpec((1, sb * t, width), lambda i: (0, i, 0))
    const = lambda shape: pl.BlockSpec(shape, lambda i: (0,) * len(shape))
    st_spec = pl.BlockSpec((sb, H_B, DK_B, DV_B), lambda i: (i, 0, 0, 0))
    return pl.pallas_call(
        functools.partial(_ret_sample_kernel, g_c, sb),
        grid=(nsteps,),
        in_specs=[row_spec(H_B * DK_B), row_spec(H_B * DK_B), row_spec(MIX_B), row_spec(MIX_B),
                  st_spec, const((H_B, t, t)), const((H_B, t, DV_B)), const((t, H_B * DK_B))],
        out_specs=(pl.BlockSpec((1, sb * t, MIX_B), lambda i: (0, i, 0)), st_spec),
        out_shape=(jax.ShapeDtypeStruct((1, db * t, MIX_B), bf16),
                   jax.ShapeDtypeStruct((db, H_B, DK_B, DV_B), f32)),
        compiler_params=_cparams(("arbitrary",)),
        name="ret_sample",
    )(qr, kr, vr, gr, state, dec, qdec, kdec)


def _mlp_kernel(x_ref, oa_ref, ob_ref, wo_ref, wu_ref, wd_ref, g1_ref, g2_ref, g3_ref, y_ref):
    mixed = _dot(oa_ref[...], wo_ref[0:MIX_A, :]) + _dot(ob_ref[...], wo_ref[MIX_A:, :])
    x1 = x_ref[...] + _rms(mixed, g1_ref[...])
    hn = _rms(x1, g2_ref[...]).astype(bf16)
    u = jnp.square(jnp.maximum(_dot(hn, wu_ref[...]), 0.0)).astype(bf16)
    y_ref[...] = x1 + _rms(_dot(u, wd_ref[...]), g3_ref[...])


def _mlp(x, oa, ob, wo, wu, wd, g1, g2, g3, tm):
    rows = x.shape[0]
    row_spec = lambda width: pl.BlockSpec((tm, width), lambda i: (i, 0))
    const = lambda shape: pl.BlockSpec(shape, lambda i: (0, 0), pipeline_mode=pl.Buffered(1))
    return pl.pallas_call(
        _mlp_kernel,
        grid=(rows // tm,),
        in_specs=[row_spec(D_MODEL), row_spec(MIX_A), row_spec(MIX_B),
                  const((D_MODEL, D_MODEL)), const((D_MODEL, D_FF)), const((D_FF, D_MODEL)),
                  const((1, D_MODEL)), const((1, D_MODEL)), const((1, D_MODEL))],
        out_specs=row_spec(D_MODEL),
        out_shape=jax.ShapeDtypeStruct((rows, D_MODEL), f32),
        compiler_params=_cparams(("arbitrary",)),
        name="merge_mlp",
    )(x, oa, ob, wo, wu, wd, g1, g2, g3)


def _tile_lanes(x, n):
    return jnp.concatenate([x] * n, axis=1)


def _suffix_sums(x):
    lane = lax.broadcasted_iota(jnp.int32, x.shape, 1)
    s = x
    k = 1
    while k < LANES:
        s = s + jnp.where(lane + k < LANES, pltpu.roll(s, LANES - k, 1), 0.0)
        k *= 2
    return s


def _mlp_sample_kernel(pg, n_pages, db, spp, pt_ref,
                       x_ref, oa_ref, ob_ref, wo_ref, wu_ref, wd_ref, g1_ref, g2_ref, g3_ref,
                       ck_hbm, cv_hbm, clf_hbm, q_ref, kn_ref, vn_ref, cn_ref,
                       y_ref, o_ref, kbuf, vbuf, lfbuf, ksem, vsem, lfsem,
                       qbd_ref, m_ref, l_ref, acc_ref, hn_ref, u_ref, down_ref):
    ng = n_pages // pg
    t = q_ref.shape[1] // spp
    ff = D_FF // ng

    def mlp_before_scores(g):
        if g == 0:
            mixed = _dot(oa_ref[...], wo_ref[0:MIX_A, :]) + _dot(ob_ref[...], wo_ref[MIX_A:, :])
            x1 = x_ref[...] + _rms(mixed, g1_ref[...])
            y_ref[...] = x1
            hn_ref[...] = _rms(x1, g2_ref[...]).astype(bf16)
        else:
            d = _dot(u_ref[...], wd_ref[(g - 1) * ff:g * ff, :])
            if g == 1:
                down_ref[...] = d
            else:
                down_ref[...] = down_ref[...] + d

    def mlp_before_softmax(g):
        u = _dot(hn_ref[...], wu_ref[:, g * ff:(g + 1) * ff])
        u_ref[...] = jnp.square(jnp.maximum(u, 0.0)).astype(bf16)

    for s in range(spp):
        seq = pl.program_id(0) * spp + s
        o_ref[s] = _sample_attend(pg, n_pages, db, seq, pt_ref, ck_hbm, cv_hbm, clf_hbm,
                                  q_ref[0, s * t:(s + 1) * t], kn_ref[0, s * t:(s + 1) * t],
                                  vn_ref[0, s * t:(s + 1) * t], cn_ref[s],
                                  kbuf, vbuf, lfbuf, ksem, vsem, lfsem,
                                  qbd_ref, m_ref, l_ref, acc_ref,
                                  (mlp_before_scores, mlp_before_softmax) if s == 0 else None
                                  ).astype(o_ref.dtype)
    y_ref[...] = y_ref[...] + _rms(down_ref[...], g3_ref[...])


def _sample_attend(pg, n_pages, db, b, pt_ref, ck_hbm, cv_hbm, clf_hbm, q, kn32, vn32, cn8,
                   kbuf, vbuf, lfbuf, ksem, vsem, lfsem,
                   qbd_ref, m_ref, l_ref, acc_ref, hooks):
    ng = n_pages // pg
    t = q.shape[0]
    nrow = t * H_A
    width = pg * PAGE

    def page_id(seq, j):
        return pt_ref[(n_pages - 1 - j) * db + seq]

    def kv_copies(pid, slot, i):
        lanes = pl.ds(i * PAGE, PAGE)
        return (pltpu.make_async_copy(ck_hbm.at[pid], kbuf.at[slot, :, :, lanes], ksem.at[slot]),
                pltpu.make_async_copy(cv_hbm.at[pid], vbuf.at[slot, :, :, lanes], vsem.at[slot]))

    def lf_copy(pid, lslot, j):
        return pltpu.make_async_copy(clf_hbm.at[pid], lfbuf.at[lslot, j], lfsem.at[lslot])

    def start_kv(seq, g, slot):
        for i in range(pg):
            for cp in kv_copies(page_id(seq, g * pg + i), slot, i):
                cp.start()

    def wait_kv(slot):
        for i in range(pg):
            for cp in kv_copies(0, slot, i):
                cp.wait()

    def start_lf(seq, lslot):
        for j in range(n_pages):
            lf_copy(page_id(seq, j), lslot, j).start()

    lslot = b % 2
    nslot = kbuf.shape[0]
    depth = nslot - 1

    @pl.when(b == 0)
    def _prime():
        start_lf(0, 0)
        for g0 in range(depth):
            start_kv(0, g0, g0)

    for j in range(n_pages):
        lf_copy(0, lslot, j).wait()

    @pl.when(b + 1 < db)
    def _next_lf():
        start_lf(b + 1, 1 - lslot)

    hsel = (lax.broadcasted_iota(jnp.int32, (H_A, MIX_A), 1) // HD_A
            == lax.broadcasted_iota(jnp.int32, (H_A, MIX_A), 0))
    qbd = jnp.where(hsel[None], q[:, None, :], 0.0)
    qbd_ref[...] = qbd.reshape(nrow, MIX_A)

    run = jnp.zeros((H_A, LANES), f32)
    for g in range(ng):
        slot = g % nslot
        ahead = g + depth
        if ahead < ng:
            start_kv(b, ahead, ahead % nslot)
        else:
            @pl.when(b + 1 < db)
            def _next_seq():
                start_kv(b + 1, ahead - ng, ahead % nslot)
        wait_kv(slot)
        if hooks is not None:
            hooks[0](g)
        biases = []
        for i in range(pg):
            lf = lfbuf[lslot, g * pg + i]
            incl = _suffix_sums(lf)
            biases.append(incl - lf + run)
            run = run + incl[:, 0:1]
        bias = jnp.concatenate(biases, axis=1) * LOG2E
        s = _dot(qbd_ref[...], kbuf[slot].reshape(MIX_A, width))
        if hooks is not None:
            hooks[1](g)
        s = (s.reshape(t, H_A, width) + bias[None]).reshape(nrow, width)
        m_cur = jnp.max(s, axis=1, keepdims=True)
        if g == 0:
            m_next = jnp.broadcast_to(m_cur, (nrow, LANES))
        else:
            m_prev = m_ref[...]
            m_next = jnp.maximum(m_prev, m_cur)
        p = jnp.exp2(s - _tile_lanes(m_next, width // LANES))
        p_sum = jnp.sum(p, axis=1, keepdims=True)
        pv = _dot_nt(p, vbuf[slot].reshape(MIX_A, width))
        if g == 0:
            l_ref[...] = jnp.broadcast_to(p_sum, (nrow, LANES))
            acc_ref[...] = pv
        else:
            a = jnp.exp2(m_prev - m_next)
            l_ref[...] = a * l_ref[...] + p_sum
            acc_ref[...] = acc_ref[...] * _tile_lanes(a, MIX_A // LANES) + pv
        m_ref[...] = m_next
    if hooks is not None:
        hooks[0](ng)

    pad = jnp.zeros((t, MIX_A), f32)
    kn = jnp.concatenate([kn32, pad], axis=0)
    vn = jnp.concatenate([vn32, pad], axis=0)
    cn = jnp.concatenate([cn8, jnp.zeros((H_A, t), f32)], axis=1) * LOG2E
    sn = _dot_nt(qbd_ref[...], kn)
    sn = (sn.reshape(t, H_A, 2 * t) - cn[None]).reshape(nrow, 2 * t)
    row_t = lax.broadcasted_iota(jnp.int32, (nrow, 2 * t), 0) // H_A
    col = lax.broadcasted_iota(jnp.int32, (nrow, 2 * t), 1)
    sn = jnp.where(col <= row_t, sn, NEG)
    m_prev = m_ref[...]
    m_next = jnp.maximum(m_prev, jnp.max(sn, axis=1, keepdims=True))
    a = jnp.exp2(m_prev - m_next)
    pn = jnp.exp2(sn - m_next[:, 0:1])
    l = a * l_ref[...] + jnp.sum(pn, axis=1, keepdims=True)
    acc = acc_ref[...] * _tile_lanes(a, MIX_A // LANES) + _dot(pn, vn)
    o = acc / _tile_lanes(l, MIX_A // LANES)
    o3 = jnp.where(hsel[None], o.reshape(t, H_A, MIX_A), 0.0)
    return jnp.sum(o3, axis=1)


def _mlp_sample(x, oa, ob, wo, wu, wd, g1, g2, g3, pt_flat, ck, cv, clf, q, kn, vn, cnt, db, t, pg, tm):
    rows = x.shape[0]
    nsteps = rows // tm
    spp = db // nsteps
    n_pages = pt_flat.shape[0] // db
    ng = n_pages // pg
    assert db % nsteps == 0 and ng % KV_SLOTS == 0 and D_FF % ng == 0
    nrow = t * H_A
    row_spec = lambda width: pl.BlockSpec((tm, width), lambda i, pt: (i, 0))
    const = lambda shape: pl.BlockSpec(shape, lambda i, pt: (0, 0), pipeline_mode=pl.Buffered(1))
    seq_spec = lambda width: pl.BlockSpec((1, spp * t, width), lambda i, pt: (0, i, 0))
    hbm = pl.BlockSpec(memory_space=pl.ANY)
    grid_spec = pltpu.PrefetchScalarGridSpec(
        num_scalar_prefetch=1,
        grid=(nsteps,),
        in_specs=[row_spec(D_MODEL), row_spec(MIX_A), row_spec(MIX_B),
                  const((D_MODEL, D_MODEL)), const((D_MODEL, D_FF)), const((D_FF, D_MODEL)),
                  const((1, D_MODEL)), const((1, D_MODEL)), const((1, D_MODEL)),
                  hbm, hbm, hbm, seq_spec(MIX_A), seq_spec(MIX_A), seq_spec(MIX_A),
                  pl.BlockSpec((spp, H_A, t), lambda i, pt: (i, 0, 0))],
        out_specs=(row_spec(D_MODEL), pl.BlockSpec((spp, t, MIX_A), lambda i, pt: (i, 0, 0))),
        scratch_shapes=[
            pltpu.VMEM((KV_SLOTS, H_A, HD_A, pg * PAGE), f32),
            pltpu.VMEM((KV_SLOTS, H_A, HD_A, pg * PAGE), f32),
            pltpu.VMEM((2, n_pages, H_A, PAGE), f32),
            pltpu.SemaphoreType.DMA((KV_SLOTS,)),
            pltpu.SemaphoreType.DMA((KV_SLOTS,)),
            pltpu.SemaphoreType.DMA((2,)),
            pltpu.VMEM((nrow, MIX_A), f32),
            pltpu.VMEM((nrow, LANES), f32),
            pltpu.VMEM((nrow, LANES), f32),
            pltpu.VMEM((nrow, MIX_A), f32),
            pltpu.VMEM((tm, D_MODEL), bf16),
            pltpu.VMEM((tm, D_FF // ng), bf16),
            pltpu.VMEM((tm, D_MODEL), f32),
        ],
    )
    return pl.pallas_call(
        functools.partial(_mlp_sample_kernel, pg, n_pages, db, spp),
        grid_spec=grid_spec,
        out_shape=(jax.ShapeDtypeStruct((rows, D_MODEL), f32),
                   jax.ShapeDtypeStruct((db, t, MIX_A), bf16)),
        compiler_params=_cparams(("arbitrary",)),
        name="mlp_sample",
    )(pt_flat, x, oa, ob, wo, wu, wd, g1, g2, g3, ck, cv, clf, q, kn, vn, cnt)


def _largest_divisor(n, candidates):
    for c in candidates:
        if n % c == 0:
            return c
    raise ValueError(f"no tile size for {n}")


def _rope_tables(pos):
    half = DK_B // 2
    inv = ROPE_BASE ** (-jnp.arange(half, dtype=f32) / half)
    ang = pos[:, None] * inv[None, :]
    cos = jnp.cos(ang)
    sin = jnp.sin(ang)
    return (jnp.concatenate([cos, cos, cos, cos], axis=1),
            jnp.concatenate([-sin, sin, -sin, sin], axis=1))


def _sum_matrix(seg_id, suffix):
    i = np.arange(seg_id.shape[0])
    same = seg_id[:, None] == seg_id[None, :]
    prefix = same & (i[None, :] <= i[:, None]) & ~suffix[:, None]
    later = same & (i[None, :] > i[:, None]) & suffix[:, None]
    return jnp.asarray(prefix.astype(np.float32) - later.astype(np.float32), bf16)


def kernel(x_prompt, x_sample, cache_k, cache_v, cache_logf, state_ret, page_table, meta_tokens,
           g_pre_mix, w_in, b_f, w_out, g_post_mix, g_pre_mlp, w_up, w_down, g_post_mlp):
    nb, seq, _ = x_prompt.shape
    db, t, _ = x_sample.shape
    n_pool = cache_k.shape[1]
    n_pages = page_table.shape[1]
    n_s = db * t
    assert w_in.shape[0] == 1, "single layer"
    assert seq % 256 == 0 and n_s % LANES == 0 and cache_k.shape[2] == PAGE and n_pages % 16 == 0

    w = w_in[0]
    n_fa = 3 * MIX_A
    w_p = jnp.concatenate([w[:, :n_fa], w[:, n_fa + H_A:], w[:, n_fa:n_fa + H_A],
                           jnp.zeros((D_MODEL, D_INP - w.shape[1]), w.dtype)], axis=1).astype(bf16)
    bf_p = jnp.concatenate([b_f[0], jnp.zeros((LANES - H_A,), f32)])[None]
    g0 = g_pre_mix[0][None]
    wo = w_out[0].astype(bf16)
    wu = w_up[0].astype(bf16)
    wd = w_down[0].astype(bf16)
    g1, g2, g3 = g_post_mix[0][None], g_pre_mlp[0][None], g_post_mlp[0][None]

    rows_aux = n_s + LANES
    n_pad = LANES - N_META
    x_aux = jnp.concatenate([x_sample.reshape(n_s, D_MODEL), meta_tokens,
                             jnp.zeros((n_pad, D_MODEL), f32)], axis=0)[None]
    past = n_pages * PAGE
    pos_aux = jnp.concatenate([jnp.tile(past + jnp.arange(t, dtype=f32), db),
                               jnp.arange(N_META, dtype=f32), jnp.zeros((n_pad,), f32)])
    cos_a, sin_a = _rope_tables(pos_aux)
    seg_aux = np.concatenate([np.arange(n_s) // t, np.full((N_META,), db), np.full((n_pad,), db + 1)])
    is_meta = seg_aux == db
    (q_a, kaug_a, vt_a, k32_a, v32_a, lf_a, ccol_a, qr_a, kr_a, vr_a, gr_a) = _proj(
        False, x_aux, g0, w_p, bf_p, cos_a, sin_a, _sum_matrix(seg_aux, is_meta), rows_aux)
    meta_blk16 = n_s // N_META
    meta_blk128 = n_s // LANES

    tm = _largest_divisor(seq, (512, 256))
    cos_p, sin_p = _rope_tables(N_META + jnp.arange(seq, dtype=f32))
    one_seg = np.zeros((tm,), np.int64)
    (qaug_p, kaug_p, vt_p, k32_p, v32_p, lf_p, qr_p, kr_p, vr_p, gr_p) = _proj(
        True, x_prompt, g0, w_p, bf_p, cos_p, sin_p, _sum_matrix(one_seg, one_seg > 0), tm,
        meta=(k32_a, v32_a, lf_a, meta_blk16))

    oa_p = _fox_prompt(qaug_p, kaug_p, vt_p, kaug_a, vt_a, meta_blk128, min(tm, Q_TILE))
    ob_p, s_p = _ret_prompt(qr_p, kr_p, vr_p, gr_p, kr_a, vr_a, meta_blk16, RET_CHUNK)

    cnt = ccol_a[0, :n_s].reshape(db, t, H_A).transpose(0, 2, 1)
    pg = KV_GROUP
    tm_mlp = 256
    y_p, oa_s = _mlp_sample(x_prompt.reshape(nb * seq, D_MODEL), oa_p.reshape(nb * seq, MIX_A),
                            ob_p.reshape(nb * seq, MIX_B), wo, wu, wd, g1, g2, g3,
                            page_table.T.reshape(n_pages * db),
                            cache_k[0].transpose(0, 2, 3, 1), cache_v[0].transpose(0, 2, 3, 1),
                            cache_logf[0].transpose(0, 2, 1),
                            q_a, k32_a, v32_a, cnt, db, t, pg, tm_mlp)

    sb = _largest_divisor(db, (8, 4, 2))
    ob_s, s_s = _ret_sample(qr_a, kr_a, vr_a, gr_a, state_ret[0], db, t, sb)
    tms = _largest_divisor(n_s, (512, 256, 128))
    y_s = _mlp(x_sample.reshape(n_s, D_MODEL), oa_s.reshape(n_s, MIX_A), ob_s[0],
               wo, wu, wd, g1, g2, g3, tms)

    k_prompt = k32_p.reshape(1, nb, N_META + seq, H_A, HD_A)
    v_prompt = v32_p.reshape(1, nb, N_META + seq, H_A, HD_A)
    return (y_p.reshape(nb, seq, D_MODEL), y_s.reshape(db, t, D_MODEL),
            k_prompt, v_prompt, lf_p[None], s_p[None],
            k32_a[0, :n_s].reshape(1, db, t, H_A, HD_A), v32_a[0, :n_s].reshape(1, db, t, H_A, HD_A),
            lf_a[0, :n_s].reshape(1, db, t, H_A), s_s[None])
```
